```python
import math
import jax, jax.numpy as jnp
from jax import lax
import numpy as np

D_MODEL = 2048
BATCH = 1
SEQ = 16384
DEPTH = 1
DEC_BATCH = 1
DEC_SEQ = 8192
PAST_LEN = 128

GRID_W = 64
NA_HEADS = 8
NA_HEAD_DIM = 128
NA_WIDTH = NA_HEADS * NA_HEAD_DIM
NA_MAX_ROWS = 8
NA_COLS = 16
NA_QBLOCK = NA_COLS
NA_KBLOCK = 2 * NA_COLS
ML_HEADS = 4
ML_HEAD_DIM = 256
ML_WIDTH = ML_HEADS * ML_HEAD_DIM
ML_CHUNK = 64
CONV_W = 3
N_ML_GATES = 4 * ML_HEADS
N_GROUPS = 4
EXPERTS_PER_GROUP = 4
N_EXPERTS = N_GROUPS * EXPERTS_PER_GROUP
TOP_K = 2
D_EXPERT = 512
EPS = 1e-6
N_IN = 3 * NA_WIDTH + 4 * ML_WIDTH + N_ML_GATES + 2 * D_MODEL

kernel_name = "hybrid_natten_mlstm_hmoe_encoder"


def _rmsnorm(x, g):
    xf = x.astype(jnp.float32)
    y = xf * lax.rsqrt(jnp.mean(xf * xf, axis=-1, keepdims=True) + EPS)
    return (y * g.astype(jnp.float32)).astype(x.dtype)


def _na_col_tables():
    n_blk = GRID_W // NA_QBLOCK
    c = np.arange(GRID_W)
    start_c = np.clip(c - NA_COLS // 2, 0, GRID_W - NA_COLS)
    blk_start = np.clip(np.arange(n_blk) * NA_QBLOCK - NA_COLS // 2, 0, GRID_W - NA_KBLOCK)
    key_cols = blk_start[:, None] + np.arange(NA_KBLOCK)
    kc = key_cols[c // NA_QBLOCK]
    mask = (kc >= start_c[:, None]) & (kc < start_c[:, None] + NA_COLS)
    dc = np.clip(kc - c[:, None] + NA_COLS - 1, 0, 2 * NA_COLS - 2)
    shp = (n_blk, NA_QBLOCK, NA_KBLOCK)
    return key_cols, mask.reshape(shp), dc.reshape(shp)


def _neighbourhood_attention(q, k, v, rpb):
    B, S, H, dh = q.shape
    rows = S // GRID_W
    kr = min(NA_MAX_ROWS, rows)
    n_blk = GRID_W // NA_QBLOCK
    key_cols, mask, dc_idx = _na_col_tables()
    qg = q.reshape(B, rows, GRID_W, H, dh)
    kg = k.reshape(B, rows, GRID_W, H, dh)
    vg = v.reshape(B, rows, GRID_W, H, dh)
    scale = dh ** -0.5

    def row_step(r):
        rs = jnp.clip(r - kr // 2, 0, rows - kr)
        k_rows = lax.dynamic_slice_in_dim(kg, rs, kr, axis=1)
        v_rows = lax.dynamic_slice_in_dim(vg, rs, kr, axis=1)
        k_blk = k_rows[:, :, key_cols]
        v_blk = v_rows[:, :, key_cols]
        q_blk = lax.dynamic_index_in_dim(qg, r, axis=1, keepdims=False).reshape(B, n_blk, NA_QBLOCK, H, dh)
        s = jnp.einsum('bjqhd,brjkhd->bhjqrk', q_blk, k_blk, preferred_element_type=jnp.float32) * scale
        dr_idx = rs + jnp.arange(kr) - r + NA_MAX_ROWS - 1
        bias = rpb[:, dr_idx][:, :, dc_idx]
        s = s + jnp.transpose(bias, (0, 2, 3, 1, 4)).astype(jnp.float32)
        s = jnp.where(mask[:, :, None, :], s, -jnp.inf)
        p = jax.nn.softmax(s.reshape(B, H, n_blk, NA_QBLOCK, kr * NA_KBLOCK), axis=-1)
        p = p.reshape(s.shape).astype(v.dtype)
        o = jnp.einsum('bhjqrk,brjkhd->bjqhd', p, v_blk)
        return o.reshape(B, GRID_W, H, dh)

    out = lax.map(row_step, jnp.arange(rows))
    return jnp.moveaxis(out, 0, 1).reshape(B, S, H * dh)


def _centred_conv(x, w, b):
    half = CONV_W // 2
    S = x.shape[1]
    xp = jnp.pad(x, ((0, 0), (half, half), (0, 0)))
    y = b
    for j in range(CONV_W):
        y = y + xp[:, j:j + S] * w[j]
    return y


def _mlstm_scan(q, k, v, log_i, log_f):
    B, H, S, dh = q.shape
    nc = S // ML_CHUNK

    def to_chunks(a):
        return jnp.moveaxis(a.reshape(a.shape[:2] + (nc, ML_CHUNK) + a.shape[3:]), 2, 0)

    tri = jnp.tril(jnp.ones((ML_CHUNK, ML_CHUNK), dtype=bool))

    def step(carry, inp):
        C, n, m = carry
        qc, kc, vc, ic, fc = inp
        b = jnp.cumsum(fc, axis=-1)
        log_d = jnp.where(tri, b[..., :, None] - b[..., None, :] + ic[..., None, :], -jnp.inf)
        m_t = jnp.maximum(b + m[..., None], jnp.max(log_d, axis=-1))
        d = jnp.exp(log_d - m_t[..., None])
        inter = jnp.exp(b + m[..., None] - m_t)
        sd = jnp.einsum('bhtd,bhjd->bhtj', qc, kc) * d
        num = jnp.einsum('bhtj,bhje->bhte', sd, vc) + inter[..., None] * jnp.einsum('bhtd,bhde->bhte', qc, C)
        den = jnp.sum(sd, axis=-1) + inter * jnp.einsum('bhtd,bhd->bht', qc, n)
        h = num / jnp.maximum(jnp.abs(den), jnp.exp(-m_t))[..., None]
        g = b[..., -1]
        log_w = g[..., None] - b + ic
        m_new = jnp.maximum(g + m, jnp.max(log_w, axis=-1))
        w = jnp.exp(log_w - m_new[..., None])
        decay = jnp.exp(g + m - m_new)
        C = decay[..., None, None] * C + jnp.einsum('bhj,bhjd,bhje->bhde', w, kc, vc)
        n = decay[..., None] * n + jnp.einsum('bhj,bhjd->bhd', w, kc)
        return (C, n, m_new), h

    init = (jnp.zeros((B, H, dh, dh), jnp.float32), jnp.zeros((B, H, dh), jnp.float32),
            jnp.zeros((B, H), jnp.float32))
    _, h = lax.scan(step, init, (to_chunks(q), to_chunks(k), to_chunks(v), to_chunks(log_i), to_chunks(log_f)))
    return jnp.moveaxis(h, 0, 2).reshape(B, H, S, dh)


def _mlstm_branch(q, k, v, o, gates, conv_w, conv_b, hnorm_g):
    B, S, _ = q.shape
    qk = jax.nn.silu(_centred_conv(jnp.concatenate([q, k], axis=-1), conv_w, conv_b))
    q, k = qk[..., :ML_WIDTH], qk[..., ML_WIDTH:]

    def heads(a):
        return jnp.transpose(a.reshape(B, S, ML_HEADS, ML_HEAD_DIM), (0, 2, 1, 3)).astype(jnp.float32)

    qh, kh, vh = heads(q), heads(k) * (ML_HEAD_DIM ** -0.5), heads(v)
    gt = jnp.transpose(gates.astype(jnp.float32).reshape(B, S, 4, ML_HEADS), (2, 0, 3, 1))
    log_i_f, log_f_f = gt[0], jax.nn.log_sigmoid(gt[1])
    log_i_b, log_f_b = gt[2], jax.nn.log_sigmoid(gt[3])
    flip = lambda a: jnp.flip(a, axis=2)
    h_f = _mlstm_scan(qh, kh, vh, log_i_f, log_f_f)
    h_b = flip(_mlstm_scan(flip(qh), flip(kh), flip(vh), flip(log_i_b), flip(log_f_b)))
    h = jnp.transpose(h_f + h_b, (0, 2, 1, 3))
    h = _rmsnorm(h, hnorm_g).reshape(B, S, ML_WIDTH)
    return (h * jax.nn.sigmoid(o.astype(jnp.float32))).astype(q.dtype)


def _hier_moe(x, w_rg, b_rg, w_re, b_re, w_gate, w_up, w_down):
    B, S, D = x.shape
    xt = x.reshape(-1, D)
    pg = jax.nn.softmax((xt @ w_rg).astype(jnp.float32) + b_rg.astype(jnp.float32), axis=-1)
    p_group, g_idx = lax.top_k(pg, 1)
    le = ((xt @ w_re).astype(jnp.float32) + b_re.astype(jnp.float32)).reshape(-1, N_GROUPS, EXPERTS_PER_GROUP)
    le_sel = jnp.take_along_axis(le, g_idx[:, :, None], axis=1)[:, 0]
    pe = jax.nn.softmax(le_sel, axis=-1)
    p_top, e_idx = lax.top_k(pe, TOP_K)
    wts = p_group * p_top / jnp.sum(p_top, axis=-1, keepdims=True)
    expert_id = g_idx * EXPERTS_PER_GROUP + e_idx
    combine = jnp.sum(jax.nn.one_hot(expert_id, N_EXPERTS, dtype=jnp.float32) * wts[..., None], axis=1)
    y = jnp.zeros(xt.shape, jnp.float32)
    for e in range(N_EXPERTS):
        h = jax.nn.silu(xt @ w_gate[e]) * (xt @ w_up[e])
        y = y + combine[:, e:e + 1] * (h @ w_down[e]).astype(jnp.float32)
    return y.astype(x.dtype).reshape(B, S, D)


def _layer(x, norm1_g, w_in, b_ml_gates, b_branch, qn_g, kn_g, na_rpb, ml_conv_w, ml_conv_b,
           ml_hnorm_g, w_na_out, w_ml_out, w_o, norm2_g, w_router_group, b_router_group,
           w_router_expert, b_router_expert, w_gate, w_up, w_down):
    B, S, D = x.shape
    xn = _rmsnorm(x, norm1_g)
    proj = xn @ w_in
    cuts = [int(c) for c in np.cumsum([NA_WIDTH] * 3 + [ML_WIDTH] * 4 + [N_ML_GATES])]
    na_q, na_k, na_v, ml_q, ml_k, ml_v, ml_o, ml_g, br_g = jnp.split(proj, cuts, axis=-1)
    hs = (B, S, NA_HEADS, NA_HEAD_DIM)
    y_na = _neighbourhood_attention(_rmsnorm(na_q.reshape(hs), qn_g), _rmsnorm(na_k.reshape(hs), kn_g),
                                    na_v.reshape(hs), na_rpb)
    y_ml = _mlstm_branch(ml_q, ml_k, ml_v, ml_o, ml_g + b_ml_gates, ml_conv_w, ml_conv_b, ml_hnorm_g)
    gates = jax.nn.sigmoid((br_g + b_branch).astype(jnp.float32))
    merged = gates[..., :D] * (y_na @ w_na_out) + gates[..., D:] * (y_ml @ w_ml_out)
    x = x + merged.astype(x.dtype) @ w_o
    x = x + _hier_moe(_rmsnorm(x, norm2_g), w_router_group, b_router_group, w_router_expert,
                      b_router_expert, w_gate, w_up, w_down)
    return x


def setup_inputs(seed: int = 0) -> dict:
    key = jax.random.key(seed)
    ks = jax.random.split(key, 24)
    L = DEPTH
    nrm = lambda k, shape, scale: jax.random.normal(k, shape, jnp.float32) * scale
    f_bias = jnp.linspace(3.0, 6.0, ML_HEADS)
    zero_h = jnp.zeros((ML_HEADS,), jnp.float32)
    base = jnp.stack([zero_h, f_bias, zero_h, f_bias])
    b_ml_gates = (base[None] + nrm(ks[4], (L, 4, ML_HEADS), 0.1)).reshape(L, N_ML_GATES)
    return {
        "x_prompt": nrm(ks[0], (BATCH, SEQ, D_MODEL), 1.0),
        "x_sample": nrm(ks[1], (DEC_BATCH, DEC_SEQ, D_MODEL), 1.0),
        "norm1_g": 1.0 + nrm(ks[2], (L, D_MODEL), 0.02),
        "w_in": nrm(ks[3], (L, D_MODEL, N_IN), D_MODEL ** -0.5),
        "b_ml_gates": b_ml_gates,
        "b_branch": nrm(ks[5], (L, 2 * D_MODEL), 0.1),
        "qn_g": 1.0 + nrm(ks[6], (L, NA_HEADS, NA_HEAD_DIM), 0.02),
        "kn_g": 1.0 + nrm(ks[7], (L, NA_HEADS, NA_HEAD_DIM), 0.02),
        "na_rpb": nrm(ks[8], (L, NA_HEADS, 2 * NA_MAX_ROWS - 1, 2 * NA_COLS - 1), 0.5),
        "ml_conv_w": nrm(ks[9], (L, CONV_W, 2 * ML_WIDTH), CONV_W ** -0.5),
        "ml_conv_b": nrm(ks[10], (L, 2 * ML_WIDTH), 0.02),
        "ml_hnorm_g": 1.0 + nrm(ks[11], (L, ML_HEADS, ML_HEAD_DIM), 0.02),
        "w_na_out": nrm(ks[12], (L, NA_WIDTH, D_MODEL), NA_WIDTH ** -0.5),
        "w_ml_out": nrm(ks[13], (L, ML_WIDTH, D_MODEL), ML_WIDTH ** -0.5),
        "w_o": nrm(ks[14], (L, D_MODEL, D_MODEL), D_MODEL ** -0.5),
        "norm2_g": 1.0 + nrm(ks[15], (L, D_MODEL), 0.02),
        "w_router_group": nrm(ks[16], (L, D_MODEL, N_GROUPS), D_MODEL ** -0.5),
        "b_router_group": nrm(ks[17], (L, N_GROUPS), 0.01),
        "w_router_expert": nrm(ks[18], (L, D_MODEL, N_EXPERTS), D_MODEL ** -0.5),
        "b_router_expert": nrm(ks[19], (L, N_EXPERTS), 0.01),
        "w_gate": nrm(ks[20], (L, N_EXPERTS, D_MODEL, D_EXPERT), D_MODEL ** -0.5),
        "w_up": nrm(ks[21], (L, N_EXPERTS, D_MODEL, D_EXPERT), D_MODEL ** -0.5),
        "w_down": nrm(ks[22], (L, N_EXPERTS, D_EXPERT, D_MODEL), D_EXPERT ** -0.5),
    }


def reference(x_prompt, x_sample, norm1_g, w_in, b_ml_gates, b_branch, qn_g, kn_g, na_rpb,
              ml_conv_w, ml_conv_b, ml_hnorm_g, w_na_out, w_ml_out, w_o, norm2_g,
              w_router_group, b_router_group, w_router_expert, b_router_expert,
              w_gate, w_up, w_down):
    def trunk(x):
        for l in range(DEPTH):
            x = _layer(x, norm1_g[l], w_in[l], b_ml_gates[l], b_branch[l], qn_g[l], kn_g[l],
                       na_rpb[l], ml_conv_w[l], ml_conv_b[l], ml_hnorm_g[l], w_na_out[l],
                       w_ml_out[l], w_o[l], norm2_g[l], w_router_group[l], b_router_group[l],
                       w_router_expert[l], b_router_expert[l], w_gate[l], w_up[l], w_down[l])
        return x

    y_prompt = trunk(x_prompt)
    y_sample = trunk(x_sample)
    return (y_prompt, y_sample)
```

```python
import functools

import numpy as np
import jax
import jax.numpy as jnp
from jax import lax
from jax.experimental import pallas as pl
from jax.experimental.pallas import tpu as pltpu

F32 = jnp.float32
BF16 = jnp.bfloat16

D_MODEL = 2048
GRID_W = 64
NA_HEADS = 8
NA_HEAD_DIM = 128
NA_WIDTH = NA_HEADS * NA_HEAD_DIM
NA_ROWS = 8
NA_COLS = 16
ML_HEADS = 4
ML_HEAD_DIM = 256
ML_WIDTH = ML_HEADS * ML_HEAD_DIM
ML_CHUNK = 64
N_ML_GATES = 4 * ML_HEADS
N_GROUPS = 4
EXPERTS_PER_GROUP = 4
N_EXPERTS = N_GROUPS * EXPERTS_PER_GROUP
D_EXPERT = 512
EPS = 1e-6

LANES = 128
COL_BLOCK = 1024
N_MAIN = 3 * NA_WIDTH + 4 * ML_WIDTH + 2 * D_MODEL
CB_NA_Q, CB_NA_K, CB_NA_V, CB_ML_Q, CB_ML_K, CB_ML_V, CB_ML_O, CB_BR = 0, 1, 2, 3, 4, 5, 6, 7
NEG_BIG = -1e30
VMEM_LIMIT = 56 * 1024 * 1024


def _params(sem, vmem=VMEM_LIMIT, **kw):
    return pltpu.CompilerParams(dimension_semantics=sem, vmem_limit_bytes=vmem, **kw)


def _inproj_kernel(x_ref, g1_ref, w_ref, wg_ref, bg_ref, qkg_ref, proj_ref, gates_ref, xn_ref):
    j = pl.program_id(1)

    @pl.when(j == 0)
    def _():
        x = x_ref[...]
        ms = jnp.mean(x * x, axis=-1, keepdims=True)
        xn = ((x * lax.rsqrt(ms + EPS)) * g1_ref[...]).astype(BF16)
        xn_ref[...] = xn
        gates_ref[...] = jnp.dot(xn, wg_ref[...], preferred_element_type=F32) + bg_ref[...]

    acc = jnp.dot(xn_ref[...], w_ref[...], preferred_element_type=F32)

    @pl.when(j <= CB_NA_K)
    def _():
        for h in range(NA_HEADS):
            sl = slice(h * NA_HEAD_DIM, (h + 1) * NA_HEAD_DIM)
            a = acc[:, sl]
            ms = jnp.mean(a * a, axis=-1, keepdims=True)
            proj_ref[:, sl] = ((a * lax.rsqrt(ms + EPS)) * qkg_ref[:, sl]).astype(BF16)

    @pl.when(j > CB_NA_K)
    def _():
        proj_ref[...] = acc.astype(BF16)


def _inproj(x, g1, w_main, w_gate, b_gate, qk_gain, tm):
    S = x.shape[0]
    nj = N_MAIN // COL_BLOCK
    return pl.pallas_call(
        _inproj_kernel,
        grid=(S // tm, nj),
        in_specs=[
            pl.BlockSpec((tm, D_MODEL), lambda i, j: (i, 0)),
            pl.BlockSpec((1, D_MODEL), lambda i, j: (0, 0)),
            pl.BlockSpec((D_MODEL, COL_BLOCK), lambda i, j: (0, j)),
            pl.BlockSpec((D_MODEL, LANES), lambda i, j: (0, 0)),
            pl.BlockSpec((1, LANES), lambda i, j: (0, 0)),
            pl.BlockSpec((1, COL_BLOCK), lambda i, j: (0, jnp.minimum(j, CB_NA_K))),
        ],
        out_specs=[
            pl.BlockSpec((tm, COL_BLOCK), lambda i, j: (i, j)),
            pl.BlockSpec((tm, LANES), lambda i, j: (i, 0)),
        ],
        out_shape=[
            jax.ShapeDtypeStruct((S, N_MAIN), BF16),
            jax.ShapeDtypeStruct((S, LANES), F32),
        ],
        scratch_shapes=[pltpu.VMEM((tm, D_MODEL), BF16)],
        compiler_params=_params(("parallel", "arbitrary")),
        name="inproj",
    )(x, g1, w_main, w_gate, b_gate, qk_gain)


NA_QROWS = 8
NA_QTOK = NA_QROWS * GRID_W
NA_KTOK = NA_ROWS * GRID_W


def _na_bias_table(rpb):
    c = np.arange(GRID_W)
    start_c = np.clip(c - NA_COLS // 2, 0, GRID_W - NA_COLS)
    kc = np.arange(GRID_W)
    mask = (kc[None, :] >= start_c[:, None]) & (kc[None, :] < start_c[:, None] + NA_COLS)
    dc = np.clip(kc[None, :] - c[:, None] + NA_COLS - 1, 0, 2 * NA_COLS - 2)
    e = jnp.where(mask[None, None], rpb.astype(F32)[:, :, dc], NEG_BIG)
    dr = np.arange(NA_ROWS)[None, :] - np.arange(NA_ROWS)[:, None] + NA_ROWS - 1
    b = e[:, dr]
    b = jnp.transpose(b, (1, 0, 3, 2, 4))
    return b.reshape(NA_ROWS * NA_HEADS, GRID_W, NA_KTOK)


def _na_kernel(q_ref, kp_ref, kc_ref, kn_ref, vp_ref, vc_ref, vn_ref, bias_ref, o_ref, kw_ref, vw_ref, *, rows):
    b = pl.program_id(0)
    kw_ref[0:NA_QTOK] = kp_ref[...]
    kw_ref[NA_QTOK:2 * NA_QTOK] = kc_ref[...]
    kw_ref[2 * NA_QTOK:3 * NA_QTOK] = kn_ref[...]
    vw_ref[0:NA_QTOK] = vp_ref[...]
    vw_ref[NA_QTOK:2 * NA_QTOK] = vc_ref[...]
    vw_ref[2 * NA_QTOK:3 * NA_QTOK] = vn_ref[...]
    scale = NA_HEAD_DIM ** -0.5

    def row_body(rl, carry):
        r = b * NA_QROWS + rl
        rs = jnp.clip(r - NA_ROWS // 2, 0, rows - NA_ROWS)
        d = r - rs
        koff = pl.multiple_of((rs - (b - 1) * NA_QROWS) * GRID_W, GRID_W)
        qoff = pl.multiple_of(rl * GRID_W, GRID_W)
        for h in range(NA_HEADS):
            sl = slice(h * NA_HEAD_DIM, (h + 1) * NA_HEAD_DIM)
            q = q_ref[pl.ds(qoff, GRID_W), sl]
            k = kw_ref[pl.ds(koff, NA_KTOK), sl]
            v = vw_ref[pl.ds(koff, NA_KTOK), sl]
            s = lax.dot_general(q, k, (((1,), (1,)), ((), ())), preferred_element_type=F32)
            s = s * scale + bias_ref[d * NA_HEADS + h]
            m = jnp.max(s, axis=-1, keepdims=True)
            p = jnp.exp(s - m)
            l = jnp.sum(p, axis=-1, keepdims=True)
            o = jnp.dot(p.astype(BF16), v, preferred_element_type=F32)
            o_ref[pl.ds(qoff, GRID_W), sl] = (o / l).astype(BF16)
        return carry

    lax.fori_loop(0, NA_QROWS, row_body, 0)


def _na_attention(proj, bias_tab):
    S = proj.shape[0]
    rows = S // GRID_W
    nb = rows // NA_QROWS
    blk = (NA_QTOK, COL_BLOCK)
    prev = lambda b: jnp.maximum(b - 1, 0)
    nxt = lambda b: jnp.minimum(b + 1, nb - 1)
    return pl.pallas_call(
        functools.partial(_na_kernel, rows=rows),
        grid=(nb,),
        in_specs=[
            pl.BlockSpec(blk, lambda b: (b, CB_NA_Q)),
            pl.BlockSpec(blk, lambda b: (prev(b), CB_NA_K)),
            pl.BlockSpec(blk, lambda b: (b, CB_NA_K)),
            pl.BlockSpec(blk, lambda b: (nxt(b), CB_NA_K)),
            pl.BlockSpec(blk, lambda b: (prev(b), CB_NA_V)),
            pl.BlockSpec(blk, lambda b: (b, CB_NA_V)),
            pl.BlockSpec(blk, lambda b: (nxt(b), CB_NA_V)),
            pl.BlockSpec((NA_ROWS * NA_HEADS, GRID_W, NA_KTOK), lambda b: (0, 0, 0)),
        ],
        out_specs=pl.BlockSpec(blk, lambda b: (b, 0)),
        out_shape=jax.ShapeDtypeStruct((S, NA_WIDTH), BF16),
        scratch_shapes=[pltpu.VMEM((3 * NA_QTOK, NA_WIDTH), BF16), pltpu.VMEM((3 * NA_QTOK, NA_WIDTH), BF16)],
        compiler_params=_params(("parallel",)),
        name="na_attention",
    )(proj, proj, proj, proj, proj, proj, proj, bias_tab)


HALO = 16


def _conv_kernel(x_ref, xp_ref, xn_ref, w_ref, b_ref, o_ref, *, tb, nblk):
    i = pl.program_id(0)
    c = pl.program_id(1)
    x = x_ref[...].astype(F32)
    prev_row = jnp.where(i > 0, xp_ref[HALO - 1:HALO, :].astype(F32), 0.0)
    next_row = jnp.where(i < nblk - 1, xn_ref[0:1, :].astype(F32), 0.0)
    ridx = lax.broadcasted_iota(jnp.int32, (tb, 1), 0)
    x_prev = jnp.where(ridx == 0, prev_row, pltpu.roll(x, 1, 0))
    x_next = jnp.where(ridx == tb - 1, next_row, pltpu.roll(x, tb - 1, 0))
    y = b_ref[...] + x_prev * w_ref[0:1, :] + x * w_ref[1:2, :] + x_next * w_ref[2:3, :]
    y = y * jax.nn.sigmoid(y)
    y = jnp.where(c == 1, y * (ML_HEAD_DIM ** -0.5), y)
    o_ref[...] = y.astype(BF16)


def _conv_silu(proj, conv_w, conv_b, tb):
    S = proj.shape[0]
    nblk = S // tb
    hb = tb // HALO
    return pl.pallas_call(
        functools.partial(_conv_kernel, tb=tb, nblk=nblk),
        grid=(nblk, 2),
        in_specs=[
            pl.BlockSpec((tb, COL_BLOCK), lambda i, c: (i, CB_ML_Q + c)),
            pl.BlockSpec((HALO, COL_BLOCK), lambda i, c: (jnp.maximum(i * hb - 1, 0), CB_ML_Q + c)),
            pl.BlockSpec((HALO, COL_BLOCK), lambda i, c: (jnp.minimum((i + 1) * hb, S // HALO - 1), CB_ML_Q + c)),
            pl.BlockSpec((3, COL_BLOCK), lambda i, c: (0, c)),
            pl.BlockSpec((1, COL_BLOCK), lambda i, c: (0, c)),
        ],
        out_specs=pl.BlockSpec((tb, COL_BLOCK), lambda i, c: (i, c)),
        out_shape=jax.ShapeDtypeStruct((S, 2 * ML_WIDTH), BF16),
        compiler_params=_params(("parallel", "parallel")),
        name="conv_silu",
    )(proj, proj, proj, conv_w, conv_b)


ML_AUG = ML_HEAD_DIM + LANES


def _log_sigmoid(x):
    return jnp.minimum(x, 0.0) - jnp.log1p(jnp.exp(-jnp.abs(x)))


def _mlstm_kernel(qkf_ref, vf_ref, gcf_ref, grf_ref, qkb_ref, vb_ref, gcb_ref, grb_ref,
                  hf_ref, hb_ref, c_ref, m_ref, *, G):
    @pl.when(pl.program_id(0) == 0)
    def _():
        c_ref[...] = jnp.zeros_like(c_ref)
        m_ref[...] = jnp.zeros_like(m_ref)

    L = ML_CHUNK
    ri = lax.broadcasted_iota(jnp.int32, (L, L), 0)
    ci = lax.broadcasted_iota(jnp.int32, (L, L), 1)
    tril = ci <= ri
    triu = ci >= ri
    trilf = tril.astype(F32)
    triuf = triu.astype(F32)
    ones_blk = jnp.ones((L, LANES), BF16)
    hi = lax.Precision.HIGHEST

    def chain(dirn, cc):
        if dirn == 0:
            qk_ref, v_ref, gc_ref, gr_ref, h_ref = qkf_ref, vf_ref, gcf_ref, grf_ref, hf_ref
            lo_m, up_m, mask = trilf, triuf, tril
        else:
            qk_ref, v_ref, gc_ref, gr_ref, h_ref = qkb_ref, vb_ref, gcb_ref, grb_ref, hb_ref
            lo_m, up_m, mask = triuf, trilf, triu
        ioff = dirn * 2 * ML_HEADS
        foff = ioff + ML_HEADS
        toff = pl.multiple_of(cc * L, L)
        gcol = gc_ref[pl.ds(toff, L), :]
        grow = gr_ref[cc]
        bcol_all = jnp.dot(lo_m, _log_sigmoid(gcol), precision=hi, preferred_element_type=F32)
        brow_all = jnp.dot(_log_sigmoid(grow), up_m, precision=hi, preferred_element_type=F32)
        for h in range(ML_HEADS):
            ch = dirn * ML_HEADS + h
            m = m_ref[ch:ch + 1, 0:1]
            bcol = bcol_all[:, foff + h:foff + h + 1]
            brow = brow_all[foff + h:foff + h + 1, :]
            irow = grow[ioff + h:ioff + h + 1, :]
            icol = gcol[:, ioff + h:ioff + h + 1]
            log_d = jnp.where(mask, bcol - brow + irow, NEG_BIG)
            m_t = jnp.maximum(bcol + m, jnp.max(log_d, axis=-1, keepdims=True))
            dmat = jnp.exp(log_d - m_t)
            inter = jnp.exp(bcol + m - m_t)
            hs = slice(h * ML_HEAD_DIM, (h + 1) * ML_HEAD_DIM)
            q = qk_ref[pl.ds(toff, L), hs]
            k = qk_ref[pl.ds(toff, L), ML_WIDTH + h * ML_HEAD_DIM:ML_WIDTH + (h + 1) * ML_HEAD_DIM]
            v = v_ref[pl.ds(toff, L), hs]
            vaug = jnp.concatenate([v, ones_blk], axis=1)
            s = lax.dot_general(q, k, (((1,), (1,)), ((), ())), preferred_element_type=F32)
            sd = (s * dmat).astype(BF16)
            cst = c_ref[ch]
            num = (jnp.dot(sd, vaug, preferred_element_type=F32)
                   + inter * jnp.dot(q, cst.astype(BF16), preferred_element_type=F32))
            den = num[:, ML_HEAD_DIM:ML_HEAD_DIM + 1]
            h_ref[pl.ds(toff, L), hs] = num[:, :ML_HEAD_DIM] / jnp.maximum(jnp.abs(den), jnp.exp(-m_t))
            g = bcol[L - 1:L, :] if dirn == 0 else bcol[0:1, :]
            log_w = g - bcol + icol
            m_new = jnp.maximum(g + m, jnp.max(log_w, axis=0, keepdims=True))
            w = jnp.exp(log_w - m_new)
            decay = jnp.exp(g + m - m_new)
            kw = (k.astype(F32) * w).astype(BF16)
            upd = lax.dot_general(kw, vaug, (((0,), (0,)), ((), ())), preferred_element_type=F32)
            c_ref[ch] = decay * cst + upd
            m_ref[ch:ch + 1, :] = jnp.broadcast_to(m_new, (1, LANES))

    def chunk_body(c, carry):
        chain(0, c)
        chain(1, G - 1 - c)
        return carry

    lax.fori_loop(0, G, chunk_body, 0)


def _mlstm_scan(qk, proj, gates, gates_t, G):
    S = qk.shape[0]
    tb = G * ML_CHUNK
    nb = S // tb
    fwd = lambda i: i
    bwd = lambda i: nb - 1 - i
    in_specs = []
    for mp in (fwd, bwd):
        in_specs += [
            pl.BlockSpec((tb, 2 * ML_WIDTH), lambda i, mp=mp: (mp(i), 0)),
            pl.BlockSpec((tb, COL_BLOCK), lambda i, mp=mp: (mp(i), CB_ML_V)),
            pl.BlockSpec((tb, LANES), lambda i, mp=mp: (mp(i), 0)),
            pl.BlockSpec((G, N_ML_GATES, ML_CHUNK), lambda i, mp=mp: (mp(i), 0, 0)),
        ]
    return pl.pallas_call(
        functools.partial(_mlstm_kernel, G=G),
        grid=(nb,),
        in_specs=in_specs,
        out_specs=[
            pl.BlockSpec((tb, ML_WIDTH), lambda i: (fwd(i), 0)),
            pl.BlockSpec((tb, ML_WIDTH), lambda i: (bwd(i), 0)),
        ],
        out_shape=[jax.ShapeDtypeStruct((S, ML_WIDTH), F32), jax.ShapeDtypeStruct((S, ML_WIDTH), F32)],
        scratch_shapes=[
            pltpu.VMEM((2 * ML_HEADS, ML_HEAD_DIM, ML_AUG), F32),
            pltpu.VMEM((2 * ML_HEADS, LANES), F32),
        ],
        compiler_params=_params(("arbitrary",)),
        name="mlstm_scan",
    )(qk, proj, gates, gates_t, qk, proj, gates, gates_t)


def _merge_kernel(yna_ref, hf_ref, hb_ref, o_ref, ga_ref, gb_ref, ba_ref, bb_ref, hg_ref, wna_ref, wml_ref,
                  out_ref, yml_ref):
    n = pl.program_id(1)

    @pl.when(n == 0)
    def _():
        for h in range(ML_HEADS):
            sl = slice(h * ML_HEAD_DIM, (h + 1) * ML_HEAD_DIM)
            hh = hf_ref[:, sl] + hb_ref[:, sl]
            ms = jnp.mean(hh * hh, axis=-1, keepdims=True)
            y = (hh * lax.rsqrt(ms + EPS)) * hg_ref[:, sl]
            yml_ref[:, sl] = (y * jax.nn.sigmoid(o_ref[:, sl].astype(F32))).astype(BF16)

    ga = jax.nn.sigmoid(ga_ref[...].astype(F32) + ba_ref[...])
    gb = jax.nn.sigmoid(gb_ref[...].astype(F32) + bb_ref[...])
    a = jnp.dot(yna_ref[...], wna_ref[...], preferred_element_type=F32)
    bm = jnp.dot(yml_ref[...], wml_ref[...], preferred_element_type=F32)
    out_ref[...] = (ga * a + gb * bm).astype(BF16)


def _merge(y_na, hf, hb, proj, b_branch, hnorm_g, w_na_out, w_ml_out, tm):
    S = y_na.shape[0]
    nn = D_MODEL // COL_BLOCK
    return pl.pallas_call(
        _merge_kernel,
        grid=(S // tm, nn),
        in_specs=[
            pl.BlockSpec((tm, NA_WIDTH), lambda i, n: (i, 0)),
            pl.BlockSpec((tm, ML_WIDTH), lambda i, n: (i, 0)),
            pl.BlockSpec((tm, ML_WIDTH), lambda i, n: (i, 0)),
            pl.BlockSpec((tm, COL_BLOCK), lambda i, n: (i, CB_ML_O)),
            pl.BlockSpec((tm, COL_BLOCK), lambda i, n: (i, CB_BR + n)),
            pl.BlockSpec((tm, COL_BLOCK), lambda i, n: (i, CB_BR + nn + n)),
            pl.BlockSpec((1, COL_BLOCK), lambda i, n: (0, n)),
            pl.BlockSpec((1, COL_BLOCK), lambda i, n: (0, nn + n)),
            pl.BlockSpec((1, ML_WIDTH), lambda i, n: (0, 0)),
            pl.BlockSpec((NA_WIDTH, COL_BLOCK), lambda i, n: (0, n)),
            pl.BlockSpec((ML_WIDTH, COL_BLOCK), lambda i, n: (0, n)),
        ],
        out_specs=pl.BlockSpec((tm, COL_BLOCK), lambda i, n: (i, n)),
        out_shape=jax.ShapeDtypeStruct((S, D_MODEL), BF16),
        scratch_shapes=[pltpu.VMEM((tm, ML_WIDTH), BF16)],
        compiler_params=_params(("parallel", "arbitrary")),
        name="merge",
    )(y_na, hf, hb, proj, proj, proj, b_branch, b_branch, hnorm_g, w_na_out, w_ml_out)


def _outproj_kernel(mg_ref, x_ref, wo_ref, g2_ref, wr_ref, br_ref, x1_ref, xn_ref, ids_ref, wts_ref):
    x1 = x_ref[...] + jnp.dot(mg_ref[...], wo_ref[...], preferred_element_type=F32)
    x1_ref[...] = x1
    ms = jnp.mean(x1 * x1, axis=-1, keepdims=True)
    xn = (x1 * lax.rsqrt(ms + EPS)) * g2_ref[...]
    xn_ref[...] = xn.astype(BF16)
    logits = jnp.dot(xn, wr_ref[...], precision=lax.Precision.HIGHEST, preferred_element_type=F32) + br_ref[...]
    tm = logits.shape[0]
    lane = lax.broadcasted_iota(jnp.int32, (tm, LANES), 1)
    big = jnp.int32(1 << 20)
    gmask = lane < N_GROUPS
    lg = jnp.where(gmask, logits, NEG_BIG)
    gmax = jnp.max(lg, axis=-1, keepdims=True)
    p_group = 1.0 / jnp.sum(jnp.exp(lg - gmax), axis=-1, keepdims=True)
    g_idx = jnp.min(jnp.where(lg == gmax, lane, big), axis=-1, keepdims=True)
    e_lo = N_GROUPS + g_idx * EXPERTS_PER_GROUP
    emask = (lane >= e_lo) & (lane < e_lo + EXPERTS_PER_GROUP)
    le = jnp.where(emask, logits, NEG_BIG)
    m1 = jnp.max(le, axis=-1, keepdims=True)
    i1 = jnp.min(jnp.where(le == m1, lane, big), axis=-1, keepdims=True)
    le2 = jnp.where(lane == i1, NEG_BIG, le)
    m2 = jnp.max(le2, axis=-1, keepdims=True)
    i2 = jnp.min(jnp.where(le2 == m2, lane, big), axis=-1, keepdims=True)
    e2 = jnp.exp(m2 - m1)
    w1 = p_group / (1.0 + e2)
    w2 = p_group * e2 / (1.0 + e2)
    ids_ref[...] = jnp.where(lane == 0, i1 - N_GROUPS, jnp.where(lane == 1, i2 - N_GROUPS, 0))
    wts_ref[...] = jnp.where(lane == 0, w1, jnp.where(lane == 1, w2, 0.0))


def _outproj(merged, x, w_o, g2, w_router, b_router, tm):
    S = x.shape[0]
    return pl.pallas_call(
        _outproj_kernel,
        grid=(S // tm,),
        in_specs=[
            pl.BlockSpec((tm, D_MODEL), lambda i: (i, 0)),
            pl.BlockSpec((tm, D_MODEL), lambda i: (i, 0)),
            pl.BlockSpec((D_MODEL, D_MODEL), lambda i: (0, 0), pipeline_mode=pl.Buffered(1)),
            pl.BlockSpec((1, D_MODEL), lambda i: (0, 0)),
            pl.BlockSpec((D_MODEL, LANES), lambda i: (0, 0)),
            pl.BlockSpec((1, LANES), lambda i: (0, 0)),
        ],
        out_specs=[
            pl.BlockSpec((tm, D_MODEL), lambda i: (i, 0)),
            pl.BlockSpec((tm, D_MODEL), lambda i: (i, 0)),
            pl.BlockSpec((tm, LANES), lambda i: (i, 0)),
            pl.BlockSpec((tm, LANES), lambda i: (i, 0)),
        ],
        out_shape=[
            jax.ShapeDtypeStruct((S, D_MODEL), F32),
            jax.ShapeDtypeStruct((S, D_MODEL), BF16),
            jax.ShapeDtypeStruct((S, LANES), jnp.int32),
            jax.ShapeDtypeStruct((S, LANES), F32),
        ],
        compiler_params=_params(("parallel",)),
        name="outproj_router",
    )(merged, x, w_o, g2, w_router, b_router)


def _moe_kernel(xn_ref, x1_ref, ids_ref, wts_ref, wg_ref, wu_ref, wd_ref, out_ref, acc_ref):
    e = pl.program_id(1)

    @pl.when(e == 0)
    def _():
        acc_ref[...] = jnp.zeros_like(acc_ref)

    xn = xn_ref[...]
    hg = jnp.dot(xn, wg_ref[0], preferred_element_type=F32)
    hu = jnp.dot(xn, wu_ref[0], preferred_element_type=F32)
    hh = ((hg * jax.nn.sigmoid(hg)) * hu).astype(BF16)
    y = jnp.dot(hh, wd_ref[0], preferred_element_type=F32)
    ids = ids_ref[...]
    wts = wts_ref[...]
    cw = (jnp.where(ids[:, 0:1] == e, wts[:, 0:1], 0.0) + jnp.where(ids[:, 1:2] == e, wts[:, 1:2], 0.0))
    acc_ref[...] += cw * y

    @pl.when(e == N_EXPERTS - 1)
    def _():
        out_ref[...] = x1_ref[...] + acc_ref[...]


def _moe(xn2, x1, ids, wts, w_gate, w_up, w_down, tm):
    S = xn2.shape[0]
    return pl.pallas_call(
        _moe_kernel,
        grid=(S // tm, N_EXPERTS),
        in_specs=[
            pl.BlockSpec((tm, D_MODEL), lambda i, e: (i, 0)),
            pl.BlockSpec((tm, D_MODEL), lambda i, e: (i, 0)),
            pl.BlockSpec((tm, LANES), lambda i, e: (i, 0)),
            pl.BlockSpec((tm, LANES), lambda i, e: (i, 0)),
            pl.BlockSpec((1, D_MODEL, D_EXPERT), lambda i, e: (e, 0, 0)),
            pl.BlockSpec((1, D_MODEL, D_EXPERT), lambda i, e: (e, 0, 0)),
            pl.BlockSpec((1, D_EXPERT, D_MODEL), lambda i, e: (e, 0, 0)),
        ],
        out_specs=pl.BlockSpec((tm, D_MODEL), lambda i, e: (i, 0)),
        out_shape=jax.ShapeDtypeStruct((S, D_MODEL), F32),
        scratch_shapes=[pltpu.VMEM((tm, D_MODEL), F32)],
        compiler_params=_params(("parallel", "arbitrary")),
        name="moe_experts",
    )(xn2, x1, ids, wts, w_gate, w_up, w_down)


def _tile(S, want):
    return min(want, S)


def _layer(x, p):
    S = x.shape[0]
    proj, gates = _inproj(x, p["g1"], p["w_main"], p["w_gate_in"], p["b_gate_in"], p["qk_gain"], _tile(S, 512))
    y_na = _na_attention(proj, p["na_bias"])
    qk = _conv_silu(proj, p["conv_w"], p["conv_b"], _tile(S, 512))
    G = 8
    gates_t = jnp.transpose(gates[:, :N_ML_GATES].reshape(S // ML_CHUNK, ML_CHUNK, N_ML_GATES), (0, 2, 1))
    hf, hb = _mlstm_scan(qk, proj, gates, gates_t, G)
    merged = _merge(y_na, hf, hb, proj, p["b_branch"], p["hnorm_g"], p["w_na_out"], p["w_ml_out"], _tile(S, 512))
    x1, xn2, ids, wts = _outproj(merged, x, p["w_o"], p["g2"], p["w_router"], p["b_router"], _tile(S, 512))
    return _moe(xn2, x1, ids, wts, p["w_gate"], p["w_up"], p["w_down"], _tile(S, 512))


def _prepare(l, norm1_g, w_in, b_ml_gates, b_branch, qn_g, kn_g, na_rpb, ml_conv_w, ml_conv_b, ml_hnorm_g,
             w_na_out, w_ml_out, w_o, norm2_g, w_router_group, b_router_group, w_router_expert,
             b_router_expert, w_gate, w_up, w_down):
    g_lo = 3 * NA_WIDTH + 4 * ML_WIDTH
    g_hi = g_lo + N_ML_GATES
    w = w_in[l]
    pad = LANES - N_ML_GATES
    n_r = N_GROUPS + N_EXPERTS
    return {
        "g1": norm1_g[l].reshape(1, D_MODEL),
        "w_main": jnp.concatenate([w[:, :g_lo], w[:, g_hi:]], axis=1).astype(BF16),
        "w_gate_in": jnp.pad(w[:, g_lo:g_hi], ((0, 0), (0, pad))).astype(BF16),
        "b_gate_in": jnp.pad(b_ml_gates[l], (0, pad)).reshape(1, LANES),
        "qk_gain": jnp.concatenate([qn_g[l].reshape(-1), kn_g[l].reshape(-1)]).reshape(1, 2 * NA_WIDTH),
        "na_bias": _na_bias_table(na_rpb[l]),
        "conv_w": ml_conv_w[l],
        "conv_b": ml_conv_b[l].reshape(1, 2 * ML_WIDTH),
        "b_branch": b_branch[l].reshape(1, 2 * D_MODEL),
        "hnorm_g": ml_hnorm_g[l].reshape(1, ML_WIDTH),
        "w_na_out": w_na_out[l].astype(BF16),
        "w_ml_out": w_ml_out[l].astype(BF16),
        "w_o": w_o[l].astype(BF16),
        "g2": norm2_g[l].reshape(1, D_MODEL),
        "w_router": jnp.pad(jnp.concatenate([w_router_group[l], w_router_expert[l]], axis=1),
                            ((0, 0), (0, LANES - n_r))),
        "b_router": jnp.pad(jnp.concatenate([b_router_group[l], b_router_expert[l]]), (0, LANES - n_r)).reshape(1, LANES),
        "w_gate": w_gate[l].astype(BF16),
        "w_up": w_up[l].astype(BF16),
        "w_down": w_down[l].astype(BF16),
    }


def kernel(x_prompt, x_sample, norm1_g, w_in, b_ml_gates, b_branch, qn_g, kn_g, na_rpb, ml_conv_w, ml_conv_b,
           ml_hnorm_g, w_na_out, w_ml_out, w_o, norm2_g, w_router_group, b_router_group, w_router_expert,
           b_router_expert, w_gate, w_up, w_down):
    depth = w_in.shape[0]
    layers = [_prepare(l, norm1_g, w_in, b_ml_gates, b_branch, qn_g, kn_g, na_rpb, ml_conv_w, ml_conv_b,
                       ml_hnorm_g, w_na_out, w_ml_out, w_o, norm2_g, w_router_group, b_router_group,
                       w_router_expert, b_router_expert, w_gate, w_up, w_down) for l in range(depth)]

    def trunk(x):
        b, s, d = x.shape
        outs = []
        for bi in range(b):
            h = x[bi]
            for p in layers:
                h = _layer(h, p)
            outs.append(h)
        return jnp.stack(outs)

    return (trunk(x_prompt), trunk(x_sample))
```

```python
import functools

import numpy as np
import jax
import jax.numpy as jnp
from jax import lax
from jax.experimental import pallas as pl
from jax.experimental.pallas import tpu as pltpu

F32 = jnp.float32
BF16 = jnp.bfloat16

D_MODEL = 2048
GRID_W = 64
NA_HEADS = 8
NA_HEAD_DIM = 128
NA_WIDTH = NA_HEADS * NA_HEAD_DIM
NA_ROWS = 8
NA_COLS = 16
ML_HEADS = 4
ML_HEAD_DIM = 256
ML_WIDTH = ML_HEADS * ML_HEAD_DIM
ML_CHUNK = 64
N_ML_GATES = 4 * ML_HEADS
N_GROUPS = 4
EXPERTS_PER_GROUP = 4
N_EXPERTS = N_GROUPS * EXPERTS_PER_GROUP
D_EXPERT = 512
EPS = 1e-6

LANES = 128
COL_BLOCK = 1024
N_MAIN = 3 * NA_WIDTH + 4 * ML_WIDTH + 2 * D_MODEL
CB_NA_Q, CB_NA_K, CB_NA_V, CB_ML_Q, CB_ML_K, CB_ML_V, CB_ML_O, CB_BR = 0, 1, 2, 3, 4, 5, 6, 7
NEG_BIG = -1e30
VMEM_LIMIT = 56 * 1024 * 1024


def _params(sem, vmem=VMEM_LIMIT, **kw):
    return pltpu.CompilerParams(dimension_semantics=sem, vmem_limit_bytes=vmem, **kw)


def _inproj_kernel(x_ref, g1_ref, w_ref, wg_ref, bg_ref, qkg_ref, proj_ref, gates_ref, xn_ref):
    j = pl.program_id(1)

    @pl.when(j == 0)
    def _():
        x = x_ref[...]
        ms = jnp.mean(x * x, axis=-1, keepdims=True)
        xn = ((x * lax.rsqrt(ms + EPS)) * g1_ref[...]).astype(BF16)
        xn_ref[...] = xn
        gates_ref[...] = jnp.dot(xn, wg_ref[...], preferred_element_type=F32) + bg_ref[...]

    acc = jnp.dot(xn_ref[...], w_ref[...], preferred_element_type=F32)

    @pl.when(j <= CB_NA_K)
    def _():
        for h in range(NA_HEADS):
            sl = slice(h * NA_HEAD_DIM, (h + 1) * NA_HEAD_DIM)
            a = acc[:, sl]
            ms = jnp.mean(a * a, axis=-1, keepdims=True)
            proj_ref[:, sl] = ((a * lax.rsqrt(ms + EPS)) * qkg_ref[:, sl]).astype(BF16)

    @pl.when(j > CB_NA_K)
    def _():
        proj_ref[...] = acc.astype(BF16)


def _inproj(x, g1, w_main, w_gate, b_gate, qk_gain, tm):
    S = x.shape[0]
    nj = N_MAIN // COL_BLOCK
    return pl.pallas_call(
        _inproj_kernel,
        grid=(S // tm, nj),
        in_specs=[
            pl.BlockSpec((tm, D_MODEL), lambda i, j: (i, 0)),
            pl.BlockSpec((1, D_MODEL), lambda i, j: (0, 0)),
            pl.BlockSpec((D_MODEL, COL_BLOCK), lambda i, j: (0, j)),
            pl.BlockSpec((D_MODEL, LANES), lambda i, j: (0, 0)),
            pl.BlockSpec((1, LANES), lambda i, j: (0, 0)),
            pl.BlockSpec((1, COL_BLOCK), lambda i, j: (0, jnp.minimum(j, CB_NA_K))),
        ],
        out_specs=[
            pl.BlockSpec((tm, COL_BLOCK), lambda i, j: (i, j)),
            pl.BlockSpec((tm, LANES), lambda i, j: (i, 0)),
        ],
        out_shape=[
            jax.ShapeDtypeStruct((S, N_MAIN), BF16),
            jax.ShapeDtypeStruct((S, LANES), F32),
        ],
        scratch_shapes=[pltpu.VMEM((tm, D_MODEL), BF16)],
        compiler_params=_params(("parallel", "arbitrary")),
        name="inproj",
    )(x, g1, w_main, w_gate, b_gate, qk_gain)


NA_QROWS = 8
NA_QTOK = NA_QROWS * GRID_W
NA_KTOK = NA_ROWS * GRID_W


def _na_bias_table(rpb):
    c = np.arange(GRID_W)
    start_c = np.clip(c - NA_COLS // 2, 0, GRID_W - NA_COLS)
    kc = np.arange(GRID_W)
    mask = (kc[None, :] >= start_c[:, None]) & (kc[None, :] < start_c[:, None] + NA_COLS)
    dc = np.clip(kc[None, :] - c[:, None] + NA_COLS - 1, 0, 2 * NA_COLS - 2)
    e = jnp.where(mask[None, None], rpb.astype(F32)[:, :, dc], NEG_BIG)
    dr = np.arange(NA_ROWS)[None, :] - np.arange(NA_ROWS)[:, None] + NA_ROWS - 1
    b = e[:, dr]
    b = jnp.transpose(b, (1, 0, 3, 2, 4))
    return b.reshape(NA_ROWS * NA_HEADS, GRID_W, NA_KTOK)


def _na_kernel(q_ref, kp_ref, kc_ref, kn_ref, vp_ref, vc_ref, vn_ref, bias_ref, o_ref, kw_ref, vw_ref, *, rows):
    b = pl.program_id(0)
    kw_ref[0:NA_QTOK] = kp_ref[...]
    kw_ref[NA_QTOK:2 * NA_QTOK] = kc_ref[...]
    kw_ref[2 * NA_QTOK:3 * NA_QTOK] = kn_ref[...]
    vw_ref[0:NA_QTOK] = vp_ref[...]
    vw_ref[NA_QTOK:2 * NA_QTOK] = vc_ref[...]
    vw_ref[2 * NA_QTOK:3 * NA_QTOK] = vn_ref[...]
    scale = NA_HEAD_DIM ** -0.5

    def row_body(rl, carry):
        r = b * NA_QROWS + rl
        rs = jnp.clip(r - NA_ROWS // 2, 0, rows - NA_ROWS)
        d = r - rs
        koff = pl.multiple_of((rs - (b - 1) * NA_QROWS) * GRID_W, GRID_W)
        qoff = pl.multiple_of(rl * GRID_W, GRID_W)
        for h in range(NA_HEADS):
            sl = slice(h * NA_HEAD_DIM, (h + 1) * NA_HEAD_DIM)
            q = q_ref[pl.ds(qoff, GRID_W), sl]
            k = kw_ref[pl.ds(koff, NA_KTOK), sl]
            v = vw_ref[pl.ds(koff, NA_KTOK), sl]
            s = lax.dot_general(q, k, (((1,), (1,)), ((), ())), preferred_element_type=F32)
            s = s * scale + bias_ref[d * NA_HEADS + h]
            m = jnp.max(s, axis=-1, keepdims=True)
            p = jnp.exp(s - m)
            l = jnp.sum(p, axis=-1, keepdims=True)
            o = jnp.dot(p.astype(BF16), v, preferred_element_type=F32)
            o_ref[pl.ds(qoff, GRID_W), sl] = (o / l).astype(BF16)
        return carry

    lax.fori_loop(0, NA_QROWS, row_body, 0)


def _na_attention(proj, bias_tab):
    S = proj.shape[0]
    rows = S // GRID_W
    nb = rows // NA_QROWS
    blk = (NA_QTOK, COL_BLOCK)
    prev = lambda b: jnp.maximum(b - 1, 0)
    nxt = lambda b: jnp.minimum(b + 1, nb - 1)
    return pl.pallas_call(
        functools.partial(_na_kernel, rows=rows),
        grid=(nb,),
        in_specs=[
            pl.BlockSpec(blk, lambda b: (b, CB_NA_Q)),
            pl.BlockSpec(blk, lambda b: (prev(b), CB_NA_K)),
            pl.BlockSpec(blk, lambda b: (b, CB_NA_K)),
            pl.BlockSpec(blk, lambda b: (nxt(b), CB_NA_K)),
            pl.BlockSpec(blk, lambda b: (prev(b), CB_NA_V)),
            pl.BlockSpec(blk, lambda b: (b, CB_NA_V)),
            pl.BlockSpec(blk, lambda b: (nxt(b), CB_NA_V)),
            pl.BlockSpec((NA_ROWS * NA_HEADS, GRID_W, NA_KTOK), lambda b: (0, 0, 0)),
        ],
        out_specs=pl.BlockSpec(blk, lambda b: (b, 0)),
        out_shape=jax.ShapeDtypeStruct((S, NA_WIDTH), BF16),
        scratch_shapes=[pltpu.VMEM((3 * NA_QTOK, NA_WIDTH), BF16), pltpu.VMEM((3 * NA_QTOK, NA_WIDTH), BF16)],
        compiler_params=_params(("parallel",)),
        name="na_attention",
    )(proj, proj, proj, proj, proj, proj, proj, bias_tab)


HALO = 16


def _conv_kernel(x_ref, xp_ref, xn_ref, w_ref, b_ref, o_ref, *, tb, nblk):
    i = pl.program_id(0)
    c = pl.program_id(1)
    x = x_ref[...].astype(F32)
    prev_row = jnp.where(i > 0, xp_ref[HALO - 1:HALO, :].astype(F32), 0.0)
    next_row = jnp.where(i < nblk - 1, xn_ref[0:1, :].astype(F32), 0.0)
    ridx = lax.broadcasted_iota(jnp.int32, (tb, 1), 0)
    x_prev = jnp.where(ridx == 0, prev_row, pltpu.roll(x, 1, 0))
    x_next = jnp.where(ridx == tb - 1, next_row, pltpu.roll(x, tb - 1, 0))
    y = b_ref[...] + x_prev * w_ref[0:1, :] + x * w_ref[1:2, :] + x_next * w_ref[2:3, :]
    y = y * jax.nn.sigmoid(y)
    y = jnp.where(c == 1, y * (ML_HEAD_DIM ** -0.5), y)
    o_ref[...] = y.astype(BF16)


def _conv_silu(proj, conv_w, conv_b, tb):
    S = proj.shape[0]
    nblk = S // tb
    hb = tb // HALO
    return pl.pallas_call(
        functools.partial(_conv_kernel, tb=tb, nblk=nblk),
        grid=(nblk, 2),
        in_specs=[
            pl.BlockSpec((tb, COL_BLOCK), lambda i, c: (i, CB_ML_Q + c)),
            pl.BlockSpec((HALO, COL_BLOCK), lambda i, c: (jnp.maximum(i * hb - 1, 0), CB_ML_Q + c)),
            pl.BlockSpec((HALO, COL_BLOCK), lambda i, c: (jnp.minimum((i + 1) * hb, S // HALO - 1), CB_ML_Q + c)),
            pl.BlockSpec((3, COL_BLOCK), lambda i, c: (0, c)),
            pl.BlockSpec((1, COL_BLOCK), lambda i, c: (0, c)),
        ],
        out_specs=pl.BlockSpec((tb, COL_BLOCK), lambda i, c: (i, c)),
        out_shape=jax.ShapeDtypeStruct((S, 2 * ML_WIDTH), BF16),
        compiler_params=_params(("parallel", "parallel")),
        name="conv_silu",
    )(proj, proj, proj, conv_w, conv_b)


ML_AUG = ML_HEAD_DIM + LANES


def _log_sigmoid(x):
    return jnp.minimum(x, 0.0) - jnp.log1p(jnp.exp(-jnp.abs(x)))


def _mlstm_kernel(qkf_ref, vf_ref, gcf_ref, grf_ref, qkb_ref, vb_ref, gcb_ref, grb_ref,
                  hf_ref, hb_ref, c_ref, m_ref, *, G):
    @pl.when(pl.program_id(0) == 0)
    def _():
        c_ref[...] = jnp.zeros_like(c_ref)
        m_ref[...] = jnp.zeros_like(m_ref)

    L = ML_CHUNK
    ri = lax.broadcasted_iota(jnp.int32, (L, L), 0)
    ci = lax.broadcasted_iota(jnp.int32, (L, L), 1)
    tril = ci <= ri
    triu = ci >= ri
    trilf = tril.astype(F32)
    triuf = triu.astype(F32)
    ones_blk = jnp.ones((L, LANES), BF16)
    hi = lax.Precision.HIGHEST

    def chain(dirn, cc):
        if dirn == 0:
            qk_ref, v_ref, gc_ref, gr_ref, h_ref = qkf_ref, vf_ref, gcf_ref, grf_ref, hf_ref
            lo_m, up_m, mask = trilf, triuf, tril
        else:
            qk_ref, v_ref, gc_ref, gr_ref, h_ref = qkb_ref, vb_ref, gcb_ref, grb_ref, hb_ref
            lo_m, up_m, mask = triuf, trilf, triu
        ioff = dirn * 2 * ML_HEADS
        foff = ioff + ML_HEADS
        toff = pl.multiple_of(cc * L, L)
        gcol = gc_ref[pl.ds(toff, L), :]
        grow = gr_ref[cc]
        bcol_all = jnp.dot(lo_m, _log_sigmoid(gcol), precision=hi, preferred_element_type=F32)
        brow_all = jnp.dot(_log_sigmoid(grow), up_m, precision=hi, preferred_element_type=F32)
        for h in range(ML_HEADS):
            ch = dirn * ML_HEADS + h
            m = m_ref[ch:ch + 1, 0:1]
            bcol = bcol_all[:, foff + h:foff + h + 1]
            brow = brow_all[foff + h:foff + h + 1, :]
            irow = grow[ioff + h:ioff + h + 1, :]
            icol = gcol[:, ioff + h:ioff + h + 1]
            log_d = jnp.where(mask, bcol - brow + irow, NEG_BIG)
            m_t = jnp.maximum(bcol + m, jnp.max(log_d, axis=-1, keepdims=True))
            dmat = jnp.exp(log_d - m_t)
            inter = jnp.exp(bcol + m - m_t)
            hs = slice(h * ML_HEAD_DIM, (h + 1) * ML_HEAD_DIM)
            q = qk_ref[pl.ds(toff, L), hs]
            k = qk_ref[pl.ds(toff, L), ML_WIDTH + h * ML_HEAD_DIM:ML_WIDTH + (h + 1) * ML_HEAD_DIM]
            v = v_ref[pl.ds(toff, L), hs]
            vaug = jnp.concatenate([v, ones_blk], axis=1)
            s = lax.dot_general(q, k, (((1,), (1,)), ((), ())), preferred_element_type=F32)
            sd = (s * dmat).astype(BF16)
            cst = c_ref[ch]
            num = (jnp.dot(sd, vaug, preferred_element_type=F32)
                   + inter * jnp.dot(q, cst.astype(BF16), preferred_element_type=F32))
            den = num[:, ML_HEAD_DIM:ML_HEAD_DIM + 1]
            h_ref[pl.ds(toff, L), hs] = num[:, :ML_HEAD_DIM] / jnp.maximum(jnp.abs(den), jnp.exp(-m_t))
            g = bcol[L - 1:L, :] if dirn == 0 else bcol[0:1, :]
            log_w = g - bcol + icol
            m_new = jnp.maximum(g + m, jnp.max(log_w, axis=0, keepdims=True))
            w = jnp.exp(log_w - m_new)
            decay = jnp.exp(g + m - m_new)
            kw = (k.astype(F32) * w).astype(BF16)
            upd = lax.dot_general(kw, vaug, (((0,), (0,)), ((), ())), preferred_element_type=F32)
            c_ref[ch] = decay * cst + upd
            m_ref[ch:ch + 1, :] = jnp.broadcast_to(m_new, (1, LANES))

    def chunk_body(c, carry):
        chain(0, c)
        chain(1, G - 1 - c)
        return carry

    lax.fori_loop(0, G, chunk_body, 0)


def _mlstm_scan(qk, proj, gates, gates_t, G):
    S = qk.shape[0]
    tb = G * ML_CHUNK
    nb = S // tb
    fwd = lambda i: i
    bwd = lambda i: nb - 1 - i
    in_specs = []
    for mp in (fwd, bwd):
        in_specs += [
            pl.BlockSpec((tb, 2 * ML_WIDTH), lambda i, mp=mp: (mp(i), 0)),
            pl.BlockSpec((tb, COL_BLOCK), lambda i, mp=mp: (mp(i), CB_ML_V)),
            pl.BlockSpec((tb, LANES), lambda i, mp=mp: (mp(i), 0)),
            pl.BlockSpec((G, N_ML_GATES, ML_CHUNK), lambda i, mp=mp: (mp(i), 0, 0)),
        ]
    return pl.pallas_call(
        functools.partial(_mlstm_kernel, G=G),
        grid=(nb,),
        in_specs=in_specs,
        out_specs=[
            pl.BlockSpec((tb, ML_WIDTH), lambda i: (fwd(i), 0)),
            pl.BlockSpec((tb, ML_WIDTH), lambda i: (bwd(i), 0)),
        ],
        out_shape=[jax.ShapeDtypeStruct((S, ML_WIDTH), F32), jax.ShapeDtypeStruct((S, ML_WIDTH), F32)],
        scratch_shapes=[
            pltpu.VMEM((2 * ML_HEADS, ML_HEAD_DIM, ML_AUG), F32),
            pltpu.VMEM((2 * ML_HEADS, LANES), F32),
        ],
        compiler_params=_params(("arbitrary",)),
        name="mlstm_scan",
    )(qk, proj, gates, gates_t, qk, proj, gates, gates_t)


def _merge_kernel(yna_ref, hf_ref, hb_ref, o_ref, ga_ref, gb_ref, ba_ref, bb_ref, hg_ref, wna_ref, wml_ref,
                  out_ref, yml_ref):
    n = pl.program_id(1)

    @pl.when(n == 0)
    def _():
        for h in range(ML_HEADS):
            sl = slice(h * ML_HEAD_DIM, (h + 1) * ML_HEAD_DIM)
            hh = hf_ref[:, sl] + hb_ref[:, sl]
            ms = jnp.mean(hh * hh, axis=-1, keepdims=True)
            y = (hh * lax.rsqrt(ms + EPS)) * hg_ref[:, sl]
            yml_ref[:, sl] = (y * jax.nn.sigmoid(o_ref[:, sl].astype(F32))).astype(BF16)

    ga = jax.nn.sigmoid(ga_ref[...].astype(F32) + ba_ref[...])
    gb = jax.nn.sigmoid(gb_ref[...].astype(F32) + bb_ref[...])
    a = jnp.dot(yna_ref[...], wna_ref[...], preferred_element_type=F32)
    bm = jnp.dot(yml_ref[...], wml_ref[...], preferred_element_type=F32)
    out_ref[...] = (ga * a + gb * bm).astype(BF16)


def _merge(y_na, hf, hb, proj, b_branch, hnorm_g, w_na_out, w_ml_out, tm):
    S = y_na.shape[0]
    nn = D_MODEL // COL_BLOCK
    return pl.pallas_call(
        _merge_kernel,
        grid=(S // tm, nn),
        in_specs=[
            pl.BlockSpec((tm, NA_WIDTH), lambda i, n: (i, 0)),
            pl.BlockSpec((tm, ML_WIDTH), lambda i, n: (i, 0)),
            pl.BlockSpec((tm, ML_WIDTH), lambda i, n: (i, 0)),
            pl.BlockSpec((tm, COL_BLOCK), lambda i, n: (i, CB_ML_O)),
            pl.BlockSpec((tm, COL_BLOCK), lambda i, n: (i, CB_BR + n)),
            pl.BlockSpec((tm, COL_BLOCK), lambda i, n: (i, CB_BR + nn + n)),
            pl.BlockSpec((1, COL_BLOCK), lambda i, n: (0, n)),
            pl.BlockSpec((1, COL_BLOCK), lambda i, n: (0, nn + n)),
            pl.BlockSpec((1, ML_WIDTH), lambda i, n: (0, 0)),
            pl.BlockSpec((NA_WIDTH, COL_BLOCK), lambda i, n: (0, n)),
            pl.BlockSpec((ML_WIDTH, COL_BLOCK), lambda i, n: (0, n)),
        ],
        out_specs=pl.BlockSpec((tm, COL_BLOCK), lambda i, n: (i, n)),
        out_shape=jax.ShapeDtypeStruct((S, D_MODEL), BF16),
        scratch_shapes=[pltpu.VMEM((tm, ML_WIDTH), BF16)],
        compiler_params=_params(("parallel", "arbitrary")),
        name="merge",
    )(y_na, hf, hb, proj, proj, proj, b_branch, b_branch, hnorm_g, w_na_out, w_ml_out)


def _outproj_kernel(mg_ref, x_ref, wo_ref, g2_ref, wr_ref, br_ref, x1_ref, xn_ref, ids_ref, wts_ref):
    x1 = x_ref[...] + jnp.dot(mg_ref[...], wo_ref[...], preferred_element_type=F32)
    x1_ref[...] = x1
    ms = jnp.mean(x1 * x1, axis=-1, keepdims=True)
    xn = (x1 * lax.rsqrt(ms + EPS)) * g2_ref[...]
    xn_ref[...] = xn
    logits = jnp.dot(xn, wr_ref[...], precision=lax.Precision.HIGHEST, preferred_element_type=F32) + br_ref[...]
    tm = logits.shape[0]
    lane = lax.broadcasted_iota(jnp.int32, (tm, LANES), 1)
    big = jnp.int32(1 << 20)
    gmask = lane < N_GROUPS
    lg = jnp.where(gmask, logits, NEG_BIG)
    gmax = jnp.max(lg, axis=-1, keepdims=True)
    p_group = 1.0 / jnp.sum(jnp.exp(lg - gmax), axis=-1, keepdims=True)
    g_idx = jnp.min(jnp.where(lg == gmax, lane, big), axis=-1, keepdims=True)
    e_lo = N_GROUPS + g_idx * EXPERTS_PER_GROUP
    emask = (lane >= e_lo) & (lane < e_lo + EXPERTS_PER_GROUP)
    le = jnp.where(emask, logits, NEG_BIG)
    m1 = jnp.max(le, axis=-1, keepdims=True)
    i1 = jnp.min(jnp.where(le == m1, lane, big), axis=-1, keepdims=True)
    le2 = jnp.where(lane == i1, NEG_BIG, le)
    m2 = jnp.max(le2, axis=-1, keepdims=True)
    i2 = jnp.min(jnp.where(le2 == m2, lane, big), axis=-1, keepdims=True)
    e2 = jnp.exp(m2 - m1)
    w1 = p_group / (1.0 + e2)
    w2 = p_group * e2 / (1.0 + e2)
    ids_ref[...] = jnp.where(lane == 0, i1 - N_GROUPS, jnp.where(lane == 1, i2 - N_GROUPS, 0))
    wts_ref[...] = jnp.where(lane == 0, w1, jnp.where(lane == 1, w2, 0.0))


def _outproj(merged, x, w_o, g2, w_router, b_router, tm):
    S = x.shape[0]
    return pl.pallas_call(
        _outproj_kernel,
        grid=(S // tm,),
        in_specs=[
            pl.BlockSpec((tm, D_MODEL), lambda i: (i, 0)),
            pl.BlockSpec((tm, D_MODEL), lambda i: (i, 0)),
            pl.BlockSpec((D_MODEL, D_MODEL), lambda i: (0, 0), pipeline_mode=pl.Buffered(1)),
            pl.BlockSpec((1, D_MODEL), lambda i: (0, 0)),
            pl.BlockSpec((D_MODEL, LANES), lambda i: (0, 0)),
            pl.BlockSpec((1, LANES), lambda i: (0, 0)),
        ],
        out_specs=[
            pl.BlockSpec((tm, D_MODEL), lambda i: (i, 0)),
            pl.BlockSpec((tm, D_MODEL), lambda i: (i, 0)),
            pl.BlockSpec((tm, LANES), lambda i: (i, 0)),
            pl.BlockSpec((tm, LANES), lambda i: (i, 0)),
        ],
        out_shape=[
            jax.ShapeDtypeStruct((S, D_MODEL), F32),
            jax.ShapeDtypeStruct((S, D_MODEL), F32),
            jax.ShapeDtypeStruct((S, LANES), jnp.int32),
            jax.ShapeDtypeStruct((S, LANES), F32),
        ],
        compiler_params=_params(("parallel",)),
        name="outproj_router",
    )(merged, x, w_o, g2, w_router, b_router)


MOE_TM = 256


def _route_plan(ids, tm):
    n_pairs = ids.shape[0] * 2
    nt = n_pairs // tm + N_EXPERTS
    p_rows = nt * tm
    e = ids.reshape(-1)
    onehot = (e[:, None] == jnp.arange(N_EXPERTS, dtype=jnp.int32)[None, :]).astype(jnp.int32)
    csum = jnp.cumsum(onehot, axis=0)
    rank = jnp.sum(csum * onehot, axis=1) - 1
    cnt = csum[-1]
    tiles_e = (cnt + tm - 1) // tm
    tend = jnp.cumsum(tiles_e)
    tstart = tend - tiles_e
    dest = tstart[e] * tm + rank
    pair_of_row = jnp.full((p_rows,), -1, jnp.int32).at[dest].set(jnp.arange(n_pairs, dtype=jnp.int32))
    is_pad = pair_of_row < 0
    pad_rank = jnp.cumsum(is_pad.astype(jnp.int32)) - 1
    pair_of_row = jnp.where(is_pad, n_pairs + pad_rank, pair_of_row)
    tile_idx = jnp.arange(nt, dtype=jnp.int32)
    tile_expert = jnp.minimum(jnp.sum((tile_idx[:, None] >= tend[None, :]).astype(jnp.int32), axis=1), N_EXPERTS - 1)
    front = p_rows + jnp.arange(tm, dtype=jnp.int32)
    back = jnp.arange(tm, dtype=jnp.int32)
    return tile_expert, jnp.concatenate([front, pair_of_row, back]), dest


def _moe_tile_kernel(te_ref, rows_ref, xn_hbm, wrow_ref, wg_ref, wu_ref, wd_ref, y_hbm,
                     gbuf, ybuf, gsem, ssem, *, tm, nt, n_tok):
    s = pl.program_id(0)
    slot = s % 2
    oslot = 1 - slot

    def gather_row(tile, r, dst_slot):
        p = rows_ref[(tile + 1) * tm + r]
        tok = jnp.minimum(lax.shift_right_logical(p, 1), n_tok - 1)
        return pltpu.make_async_copy(xn_hbm.at[pl.ds(tok, 1), :], gbuf.at[dst_slot, pl.ds(r, 1), :], gsem.at[dst_slot])

    def scatter_row(tile, r, src_slot):
        p = rows_ref[(tile + 1) * tm + r]
        return pltpu.make_async_copy(ybuf.at[src_slot, pl.ds(r, 1), :], y_hbm.at[pl.ds(p, 1), :], ssem.at[src_slot])

    def wait_gather(w_slot):
        pltpu.make_async_copy(xn_hbm.at[pl.ds(0, tm), :], gbuf.at[w_slot], gsem.at[w_slot]).wait()

    def wait_scatter(w_slot):
        pltpu.make_async_copy(ybuf.at[w_slot], y_hbm.at[pl.ds(0, tm), :], ssem.at[w_slot]).wait()

    @pl.when(s == 0)
    def _():
        ybuf[1] = jnp.zeros((tm, D_MODEL), F32)
        for r in range(tm):
            gather_row(0, r, 0).start()

    wait_gather(slot)

    @pl.when(s >= 1)
    def _():
        wait_scatter(slot)

    for r in range(tm):
        gather_row(s + 1, r, oslot).start()
        scatter_row(s - 1, r, oslot).start()
    xg = gbuf[slot].astype(BF16)
    hg = jnp.dot(xg, wg_ref[0], preferred_element_type=F32)
    hu = jnp.dot(xg, wu_ref[0], preferred_element_type=F32)
    hh = ((hg * jax.nn.sigmoid(hg)) * hu).astype(BF16)
    ybuf[slot] = jnp.dot(hh, wd_ref[0], preferred_element_type=F32) * wrow_ref[...]

    @pl.when(s == nt - 1)
    def _():
        for r in range(tm):
            scatter_row(s, r, slot).start()
        wait_scatter(oslot)
        wait_scatter(slot)
        wait_gather(oslot)


def _moe_tiles(xn2, wrow, tile_expert, rows_ext, w_gate, w_up, w_down, tm):
    n_tok = xn2.shape[0]
    nt = tile_expert.shape[0]
    wspec = lambda shape: pl.BlockSpec(shape, lambda s, te, rows: (te[s], 0, 0))
    return pl.pallas_call(
        functools.partial(_moe_tile_kernel, tm=tm, nt=nt, n_tok=n_tok),
        grid_spec=pltpu.PrefetchScalarGridSpec(
            num_scalar_prefetch=2,
            grid=(nt,),
            in_specs=[
                pl.BlockSpec(memory_space=pl.ANY),
                pl.BlockSpec((tm, 1), lambda s, te, rows: (s, 0)),
                wspec((1, D_MODEL, D_EXPERT)),
                wspec((1, D_MODEL, D_EXPERT)),
                wspec((1, D_EXPERT, D_MODEL)),
            ],
            out_specs=pl.BlockSpec(memory_space=pl.ANY),
            scratch_shapes=[
                pltpu.VMEM((2, tm, D_MODEL), F32),
                pltpu.VMEM((2, tm, D_MODEL), F32),
                pltpu.SemaphoreType.DMA((2,)),
                pltpu.SemaphoreType.DMA((2,)),
            ],
        ),
        out_shape=jax.ShapeDtypeStruct(((nt + 1) * tm, D_MODEL), F32),
        compiler_params=_params(("arbitrary",), disable_bounds_checks=True),
        name="moe_tiles",
    )(tile_expert, rows_ext, xn2, wrow, w_gate, w_up, w_down)


def _combine_kernel(x1_ref, yp_ref, out_ref):
    out_ref[...] = x1_ref[...] + (yp_ref[:, :D_MODEL] + yp_ref[:, D_MODEL:])


def _combine(x1, y_pairs, tm):
    S = x1.shape[0]
    yp = y_pairs.reshape(y_pairs.shape[0] // 2, 2 * D_MODEL)
    return pl.pallas_call(
        _combine_kernel,
        grid=(S // tm,),
        in_specs=[
            pl.BlockSpec((tm, D_MODEL), lambda i: (i, 0)),
            pl.BlockSpec((tm, 2 * D_MODEL), lambda i: (i, 0)),
        ],
        out_specs=pl.BlockSpec((tm, D_MODEL), lambda i: (i, 0)),
        out_shape=jax.ShapeDtypeStruct((S, D_MODEL), F32),
        compiler_params=_params(("parallel",)),
        name="moe_combine",
    )(x1, yp)


def _moe(xn2, x1, ids, wts, w_gate, w_up, w_down):
    tm = MOE_TM
    tile_expert, rows_ext, dest = _route_plan(ids[:, :2], tm)
    wrow = jnp.zeros((tile_expert.shape[0] * tm,), F32).at[dest].set(wts[:, :2].reshape(-1)).reshape(-1, 1)
    y_pairs = _moe_tiles(xn2, wrow, tile_expert, rows_ext, w_gate, w_up, w_down, tm)
    return _combine(x1, y_pairs, _tile(x1.shape[0], 512))


def _tile(S, want):
    return min(want, S)


def _layer(x, p):
    S = x.shape[0]
    proj, gates = _inproj(x, p["g1"], p["w_main"], p["w_gate_in"], p["b_gate_in"], p["qk_gain"], _tile(S, 512))
    y_na = _na_attention(proj, p["na_bias"])
    qk = _conv_silu(proj, p["conv_w"], p["conv_b"], _tile(S, 512))
    G = 8
    gates_t = jnp.transpose(gates[:, :N_ML_GATES].reshape(S // ML_CHUNK, ML_CHUNK, N_ML_GATES), (0, 2, 1))
    hf, hb = _mlstm_scan(qk, proj, gates, gates_t, G)
    merged = _merge(y_na, hf, hb, proj, p["b_branch"], p["hnorm_g"], p["w_na_out"], p["w_ml_out"], _tile(S, 512))
    x1, xn2, ids, wts = _outproj(merged, x, p["w_o"], p["g2"], p["w_router"], p["b_router"], _tile(S, 512))
    return _moe(xn2, x1, ids, wts, p["w_gate"], p["w_up"], p["w_down"])


def _prepare(l, norm1_g, w_in, b_ml_gates, b_branch, qn_g, kn_g, na_rpb, ml_conv_w, ml_conv_b, ml_hnorm_g,
             w_na_out, w_ml_out, w_o, norm2_g, w_router_group, b_router_group, w_router_expert,
             b_router_expert, w_gate, w_up, w_down):
    g_lo = 3 * NA_WIDTH + 4 * ML_WIDTH
    g_hi = g_lo + N_ML_GATES
    w = w_in[l]
    pad = LANES - N_ML_GATES
    n_r = N_GROUPS + N_EXPERTS
    return {
        "g1": norm1_g[l].reshape(1, D_MODEL),
        "w_main": jnp.concatenate([w[:, :g_lo], w[:, g_hi:]], axis=1).astype(BF16),
        "w_gate_in": jnp.pad(w[:, g_lo:g_hi], ((0, 0), (0, pad))).astype(BF16),
        "b_gate_in": jnp.pad(b_ml_gates[l], (0, pad)).reshape(1, LANES),
        "qk_gain": jnp.concatenate([qn_g[l].reshape(-1), kn_g[l].reshape(-1)]).reshape(1, 2 * NA_WIDTH),
        "na_bias": _na_bias_table(na_rpb[l]),
        "conv_w": ml_conv_w[l],
        "conv_b": ml_conv_b[l].reshape(1, 2 * ML_WIDTH),
        "b_branch": b_branch[l].reshape(1, 2 * D_MODEL),
        "hnorm_g": ml_hnorm_g[l].reshape(1, ML_WIDTH),
        "w_na_out": w_na_out[l].astype(BF16),
        "w_ml_out": w_ml_out[l].astype(BF16),
        "w_o": w_o[l].astype(BF16),
        "g2": norm2_g[l].reshape(1, D_MODEL),
        "w_router": jnp.pad(jnp.concatenate([w_router_group[l], w_router_expert[l]], axis=1),
                            ((0, 0), (0, LANES - n_r))),
        "b_router": jnp.pad(jnp.concatenate([b_router_group[l], b_router_expert[l]]), (0, LANES - n_r)).reshape(1, LANES),
        "w_gate": w_gate[l].astype(BF16),
        "w_up": w_up[l].astype(BF16),
        "w_down": w_down[l].astype(BF16),
    }


def kernel(x_prompt, x_sample, norm1_g, w_in, b_ml_gates, b_branch, qn_g, kn_g, na_rpb, ml_conv_w, ml_conv_b,
           ml_hnorm_g, w_na_out, w_ml_out, w_o, norm2_g, w_router_group, b_router_group, w_router_expert,
           b_router_expert, w_gate, w_up, w_down):
    depth = w_in.shape[0]
    layers = [_prepare(l, norm1_g, w_in, b_ml_gates, b_branch, qn_g, kn_g, na_rpb, ml_conv_w, ml_conv_b,
                       ml_hnorm_g, w_na_out, w_ml_out, w_o, norm2_g, w_router_group, b_router_group,
                       w_router_expert, b_router_expert, w_gate, w_up, w_down) for l in range(depth)]

    def trunk(x):
        b, s, d = x.shape
        outs = []
        for bi in range(b):
            h = x[bi]
            for p in layers:
                h = _layer(h, p)
            outs.append(h)
        return jnp.stack(outs)

    return (trunk(x_prompt), trunk(x_sample))
```

```python
import functools

import numpy as np
import jax
import jax.numpy as jnp
from jax import lax
from jax.experimental import pallas as pl
from jax.experimental.pallas import tpu as pltpu

F32 = jnp.float32
BF16 = jnp.bfloat16

D_MODEL = 2048
GRID_W = 64
NA_HEADS = 8
NA_HEAD_DIM = 128
NA_WIDTH = NA_HEADS * NA_HEAD_DIM
NA_ROWS = 8
NA_COLS = 16
ML_HEADS = 4
ML_HEAD_DIM = 256
ML_WIDTH = ML_HEADS * ML_HEAD_DIM
ML_CHUNK = 64
N_ML_GATES = 4 * ML_HEADS
N_GROUPS = 4
EXPERTS_PER_GROUP = 4
N_EXPERTS = N_GROUPS * EXPERTS_PER_GROUP
D_EXPERT = 512
EPS = 1e-6

LANES = 128
COL_BLOCK = 1024
N_MAIN = 3 * NA_WIDTH + 4 * ML_WIDTH + 2 * D_MODEL
CB_NA_Q, CB_NA_K, CB_NA_V, CB_ML_Q, CB_ML_K, CB_ML_V, CB_ML_O, CB_BR = 0, 1, 2, 3, 4, 5, 6, 7
NEG_BIG = -1e30
VMEM_LIMIT = 56 * 1024 * 1024


def _params(sem, vmem=VMEM_LIMIT, **kw):
    return pltpu.CompilerParams(dimension_semantics=sem, vmem_limit_bytes=vmem, **kw)


def _inproj_kernel(x_ref, g1_ref, w_ref, wg_ref, bg_ref, qkg_ref, proj_ref, gates_ref, xn_ref):
    j = pl.program_id(1)

    @pl.when(j == 0)
    def _():
        x = x_ref[...]
        ms = jnp.mean(x * x, axis=-1, keepdims=True)
        xn = ((x * lax.rsqrt(ms + EPS)) * g1_ref[...]).astype(BF16)
        xn_ref[...] = xn
        gates_ref[...] = jnp.dot(xn, wg_ref[...], preferred_element_type=F32) + bg_ref[...]

    acc = jnp.dot(xn_ref[...], w_ref[...], preferred_element_type=F32)

    @pl.when(j <= CB_NA_K)
    def _():
        for h in range(NA_HEADS):
            sl = slice(h * NA_HEAD_DIM, (h + 1) * NA_HEAD_DIM)
            a = acc[:, sl]
            ms = jnp.mean(a * a, axis=-1, keepdims=True)
            proj_ref[:, sl] = ((a * lax.rsqrt(ms + EPS)) * qkg_ref[:, sl]).astype(BF16)

    @pl.when(j > CB_NA_K)
    def _():
        proj_ref[...] = acc.astype(BF16)


def _inproj(x, g1, w_main, w_gate, b_gate, qk_gain, tm):
    S = x.shape[0]
    nj = N_MAIN // COL_BLOCK
    return pl.pallas_call(
        _inproj_kernel,
        grid=(S // tm, nj),
        in_specs=[
            pl.BlockSpec((tm, D_MODEL), lambda i, j: (i, 0)),
            pl.BlockSpec((1, D_MODEL), lambda i, j: (0, 0)),
            pl.BlockSpec((D_MODEL, COL_BLOCK), lambda i, j: (0, j)),
            pl.BlockSpec((D_MODEL, LANES), lambda i, j: (0, 0)),
            pl.BlockSpec((1, LANES), lambda i, j: (0, 0)),
            pl.BlockSpec((1, COL_BLOCK), lambda i, j: (0, jnp.minimum(j, CB_NA_K))),
        ],
        out_specs=[
            pl.BlockSpec((tm, COL_BLOCK), lambda i, j: (i, j)),
            pl.BlockSpec((tm, LANES), lambda i, j: (i, 0)),
        ],
        out_shape=[
            jax.ShapeDtypeStruct((S, N_MAIN), BF16),
            jax.ShapeDtypeStruct((S, LANES), F32),
        ],
        scratch_shapes=[pltpu.VMEM((tm, D_MODEL), BF16)],
        compiler_params=_params(("parallel", "arbitrary")),
        name="inproj",
    )(x, g1, w_main, w_gate, b_gate, qk_gain)


NA_QROWS = 8
NA_QTOK = NA_QROWS * GRID_W
NA_KTOK = NA_ROWS * GRID_W


def _na_bias_table(rpb):
    c = np.arange(GRID_W)
    start_c = np.clip(c - NA_COLS // 2, 0, GRID_W - NA_COLS)
    kc = np.arange(GRID_W)
    mask = (kc[None, :] >= start_c[:, None]) & (kc[None, :] < start_c[:, None] + NA_COLS)
    dc = np.clip(kc[None, :] - c[:, None] + NA_COLS - 1, 0, 2 * NA_COLS - 2)
    e = jnp.where(mask[None, None], rpb.astype(F32)[:, :, dc], NEG_BIG)
    dr = np.arange(NA_ROWS)[None, :] - np.arange(NA_ROWS)[:, None] + NA_ROWS - 1
    b = e[:, dr]
    b = jnp.transpose(b, (1, 0, 3, 2, 4))
    return b.reshape(NA_ROWS * NA_HEADS, GRID_W, NA_KTOK)


def _na_kernel(q_ref, kp_ref, kc_ref, kn_ref, vp_ref, vc_ref, vn_ref, bias_ref, o_ref, kw_ref, vw_ref, *, rows):
    b = pl.program_id(0)
    kw_ref[0:NA_QTOK] = kp_ref[...]
    kw_ref[NA_QTOK:2 * NA_QTOK] = kc_ref[...]
    kw_ref[2 * NA_QTOK:3 * NA_QTOK] = kn_ref[...]
    vw_ref[0:NA_QTOK] = vp_ref[...]
    vw_ref[NA_QTOK:2 * NA_QTOK] = vc_ref[...]
    vw_ref[2 * NA_QTOK:3 * NA_QTOK] = vn_ref[...]
    scale = NA_HEAD_DIM ** -0.5

    def row_body(rl, carry):
        r = b * NA_QROWS + rl
        rs = jnp.clip(r - NA_ROWS // 2, 0, rows - NA_ROWS)
        d = r - rs
        koff = pl.multiple_of((rs - (b - 1) * NA_QROWS) * GRID_W, GRID_W)
        qoff = pl.multiple_of(rl * GRID_W, GRID_W)
        for h in range(NA_HEADS):
            sl = slice(h * NA_HEAD_DIM, (h + 1) * NA_HEAD_DIM)
            q = q_ref[pl.ds(qoff, GRID_W), sl]
            k = kw_ref[pl.ds(koff, NA_KTOK), sl]
            v = vw_ref[pl.ds(koff, NA_KTOK), sl]
            s = lax.dot_general(q, k, (((1,), (1,)), ((), ())), preferred_element_type=F32)
            s = s * scale + bias_ref[d * NA_HEADS + h]
            m = jnp.max(s, axis=-1, keepdims=True)
            p = jnp.exp(s - m)
            l = jnp.sum(p, axis=-1, keepdims=True)
            o = jnp.dot(p.astype(BF16), v, preferred_element_type=F32)
            o_ref[pl.ds(qoff, GRID_W), sl] = (o / l).astype(BF16)
        return carry

    lax.fori_loop(0, NA_QROWS, row_body, 0)


def _na_attention(proj, bias_tab):
    S = proj.shape[0]
    rows = S // GRID_W
    nb = rows // NA_QROWS
    blk = (NA_QTOK, COL_BLOCK)
    prev = lambda b: jnp.maximum(b - 1, 0)
    nxt = lambda b: jnp.minimum(b + 1, nb - 1)
    return pl.pallas_call(
        functools.partial(_na_kernel, rows=rows),
        grid=(nb,),
        in_specs=[
            pl.BlockSpec(blk, lambda b: (b, CB_NA_Q)),
            pl.BlockSpec(blk, lambda b: (prev(b), CB_NA_K)),
            pl.BlockSpec(blk, lambda b: (b, CB_NA_K)),
            pl.BlockSpec(blk, lambda b: (nxt(b), CB_NA_K)),
            pl.BlockSpec(blk, lambda b: (prev(b), CB_NA_V)),
            pl.BlockSpec(blk, lambda b: (b, CB_NA_V)),
            pl.BlockSpec(blk, lambda b: (nxt(b), CB_NA_V)),
            pl.BlockSpec((NA_ROWS * NA_HEADS, GRID_W, NA_KTOK), lambda b: (0, 0, 0)),
        ],
        out_specs=pl.BlockSpec(blk, lambda b: (b, 0)),
        out_shape=jax.ShapeDtypeStruct((S, NA_WIDTH), BF16),
        scratch_shapes=[pltpu.VMEM((3 * NA_QTOK, NA_WIDTH), BF16), pltpu.VMEM((3 * NA_QTOK, NA_WIDTH), BF16)],
        compiler_params=_params(("parallel",)),
        name="na_attention",
    )(proj, proj, proj, proj, proj, proj, proj, bias_tab)


HALO = 16


def _conv_kernel(x_ref, xp_ref, xn_ref, w_ref, b_ref, o_ref, *, tb, nblk):
    i = pl.program_id(0)
    c = pl.program_id(1)
    x = x_ref[...].astype(F32)
    prev_row = jnp.where(i > 0, xp_ref[HALO - 1:HALO, :].astype(F32), 0.0)
    next_row = jnp.where(i < nblk - 1, xn_ref[0:1, :].astype(F32), 0.0)
    ridx = lax.broadcasted_iota(jnp.int32, (tb, 1), 0)
    x_prev = jnp.where(ridx == 0, prev_row, pltpu.roll(x, 1, 0))
    x_next = jnp.where(ridx == tb - 1, next_row, pltpu.roll(x, tb - 1, 0))
    y = b_ref[...] + x_prev * w_ref[0:1, :] + x * w_ref[1:2, :] + x_next * w_ref[2:3, :]
    y = y * jax.nn.sigmoid(y)
    y = jnp.where(c == 1, y * (ML_HEAD_DIM ** -0.5), y)
    o_ref[...] = y.astype(BF16)


def _conv_silu(proj, conv_w, conv_b, tb):
    S = proj.shape[0]
    nblk = S // tb
    hb = tb // HALO
    return pl.pallas_call(
        functools.partial(_conv_kernel, tb=tb, nblk=nblk),
        grid=(nblk, 2),
        in_specs=[
            pl.BlockSpec((tb, COL_BLOCK), lambda i, c: (i, CB_ML_Q + c)),
            pl.BlockSpec((HALO, COL_BLOCK), lambda i, c: (jnp.maximum(i * hb - 1, 0), CB_ML_Q + c)),
            pl.BlockSpec((HALO, COL_BLOCK), lambda i, c: (jnp.minimum((i + 1) * hb, S // HALO - 1), CB_ML_Q + c)),
            pl.BlockSpec((3, COL_BLOCK), lambda i, c: (0, c)),
            pl.BlockSpec((1, COL_BLOCK), lambda i, c: (0, c)),
        ],
        out_specs=pl.BlockSpec((tb, COL_BLOCK), lambda i, c: (i, c)),
        out_shape=jax.ShapeDtypeStruct((S, 2 * ML_WIDTH), BF16),
        compiler_params=_params(("parallel", "parallel")),
        name="conv_silu",
    )(proj, proj, proj, conv_w, conv_b)


ML_AUG = ML_HEAD_DIM + LANES


def _log_sigmoid(x):
    return jnp.minimum(x, 0.0) - jnp.log1p(jnp.exp(-jnp.abs(x)))


def _mlstm_kernel(qkf_ref, vf_ref, gcf_ref, grf_ref, qkb_ref, vb_ref, gcb_ref, grb_ref,
                  hf_ref, hb_ref, c_ref, m_ref, *, G):
    @pl.when(pl.program_id(0) == 0)
    def _():
        c_ref[...] = jnp.zeros_like(c_ref)
        m_ref[...] = jnp.zeros_like(m_ref)

    L = ML_CHUNK
    ri = lax.broadcasted_iota(jnp.int32, (L, L), 0)
    ci = lax.broadcasted_iota(jnp.int32, (L, L), 1)
    tril = ci <= ri
    triu = ci >= ri
    trilf = tril.astype(F32)
    triuf = triu.astype(F32)
    ones_blk = jnp.ones((L, LANES), BF16)
    hi = lax.Precision.HIGHEST

    def chain(dirn, cc):
        if dirn == 0:
            qk_ref, v_ref, gc_ref, gr_ref, h_ref = qkf_ref, vf_ref, gcf_ref, grf_ref, hf_ref
            lo_m, up_m, mask = trilf, triuf, tril
        else:
            qk_ref, v_ref, gc_ref, gr_ref, h_ref = qkb_ref, vb_ref, gcb_ref, grb_ref, hb_ref
            lo_m, up_m, mask = triuf, trilf, triu
        ioff = dirn * 2 * ML_HEADS
        foff = ioff + ML_HEADS
        toff = pl.multiple_of(cc * L, L)
        gcol = gc_ref[pl.ds(toff, L), :]
        grow = gr_ref[cc]
        bcol_all = jnp.dot(lo_m, _log_sigmoid(gcol), precision=hi, preferred_element_type=F32)
        brow_all = jnp.dot(_log_sigmoid(grow), up_m, precision=hi, preferred_element_type=F32)
        for h in range(ML_HEADS):
            ch = dirn * ML_HEADS + h
            m = m_ref[ch:ch + 1, 0:1]
            bcol = bcol_all[:, foff + h:foff + h + 1]
            brow = brow_all[foff + h:foff + h + 1, :]
            irow = grow[ioff + h:ioff + h + 1, :]
            icol = gcol[:, ioff + h:ioff + h + 1]
            log_d = jnp.where(mask, bcol - brow + irow, NEG_BIG)
            m_t = jnp.maximum(bcol + m, jnp.max(log_d, axis=-1, keepdims=True))
            dmat = jnp.exp(log_d - m_t)
            inter = jnp.exp(bcol + m - m_t)
            hs = slice(h * ML_HEAD_DIM, (h + 1) * ML_HEAD_DIM)
            q = qk_ref[pl.ds(toff, L), hs]
            k = qk_ref[pl.ds(toff, L), ML_WIDTH + h * ML_HEAD_DIM:ML_WIDTH + (h + 1) * ML_HEAD_DIM]
            v = v_ref[pl.ds(toff, L), hs]
            vaug = jnp.concatenate([v, ones_blk], axis=1)
            s = lax.dot_general(q, k, (((1,), (1,)), ((), ())), preferred_element_type=F32)
            sd = (s * dmat).astype(BF16)
            cst = c_ref[ch]
            num = (jnp.dot(sd, vaug, preferred_element_type=F32)
                   + inter * jnp.dot(q, cst.astype(BF16), preferred_element_type=F32))
            den = num[:, ML_HEAD_DIM:ML_HEAD_DIM + 1]
            h_ref[pl.ds(toff, L), hs] = num[:, :ML_HEAD_DIM] / jnp.maximum(jnp.abs(den), jnp.exp(-m_t))
            g = bcol[L - 1:L, :] if dirn == 0 else bcol[0:1, :]
            log_w = g - bcol + icol
            m_new = jnp.maximum(g + m, jnp.max(log_w, axis=0, keepdims=True))
            w = jnp.exp(log_w - m_new)
            decay = jnp.exp(g + m - m_new)
            kw = (k.astype(F32) * w).astype(BF16)
            upd = lax.dot_general(kw, vaug, (((0,), (0,)), ((), ())), preferred_element_type=F32)
            c_ref[ch] = decay * cst + upd
            m_ref[ch:ch + 1, :] = jnp.broadcast_to(m_new, (1, LANES))

    def chunk_body(c, carry):
        chain(0, c)
        chain(1, G - 1 - c)
        return carry

    lax.fori_loop(0, G, chunk_body, 0)


def _mlstm_scan(qk, proj, gates, gates_t, G):
    S = qk.shape[0]
    tb = G * ML_CHUNK
    nb = S // tb
    fwd = lambda i: i
    bwd = lambda i: nb - 1 - i
    in_specs = []
    for mp in (fwd, bwd):
        in_specs += [
            pl.BlockSpec((tb, 2 * ML_WIDTH), lambda i, mp=mp: (mp(i), 0)),
            pl.BlockSpec((tb, COL_BLOCK), lambda i, mp=mp: (mp(i), CB_ML_V)),
            pl.BlockSpec((tb, LANES), lambda i, mp=mp: (mp(i), 0)),
            pl.BlockSpec((G, N_ML_GATES, ML_CHUNK), lambda i, mp=mp: (mp(i), 0, 0)),
        ]
    return pl.pallas_call(
        functools.partial(_mlstm_kernel, G=G),
        grid=(nb,),
        in_specs=in_specs,
        out_specs=[
            pl.BlockSpec((tb, ML_WIDTH), lambda i: (fwd(i), 0)),
            pl.BlockSpec((tb, ML_WIDTH), lambda i: (bwd(i), 0)),
        ],
        out_shape=[jax.ShapeDtypeStruct((S, ML_WIDTH), F32), jax.ShapeDtypeStruct((S, ML_WIDTH), F32)],
        scratch_shapes=[
            pltpu.VMEM((2 * ML_HEADS, ML_HEAD_DIM, ML_AUG), F32),
            pltpu.VMEM((2 * ML_HEADS, LANES), F32),
        ],
        compiler_params=_params(("arbitrary",)),
        name="mlstm_scan",
    )(qk, proj, gates, gates_t, qk, proj, gates, gates_t)


def _merge_kernel(yna_ref, hf_ref, hb_ref, o_ref, ga_ref, gb_ref, ba_ref, bb_ref, hg_ref, wna_ref, wml_ref,
                  out_ref, yml_ref):
    n = pl.program_id(1)

    @pl.when(n == 0)
    def _():
        for h in range(ML_HEADS):
            sl = slice(h * ML_HEAD_DIM, (h + 1) * ML_HEAD_DIM)
            hh = hf_ref[:, sl] + hb_ref[:, sl]
            ms = jnp.mean(hh * hh, axis=-1, keepdims=True)
            y = (hh * lax.rsqrt(ms + EPS)) * hg_ref[:, sl]
            yml_ref[:, sl] = (y * jax.nn.sigmoid(o_ref[:, sl].astype(F32))).astype(BF16)

    ga = jax.nn.sigmoid(ga_ref[...].astype(F32) + ba_ref[...])
    gb = jax.nn.sigmoid(gb_ref[...].astype(F32) + bb_ref[...])
    a = jnp.dot(yna_ref[...], wna_ref[...], preferred_element_type=F32)
    bm = jnp.dot(yml_ref[...], wml_ref[...], preferred_element_type=F32)
    out_ref[...] = (ga * a + gb * bm).astype(BF16)


def _merge(y_na, hf, hb, proj, b_branch, hnorm_g, w_na_out, w_ml_out, tm):
    S = y_na.shape[0]
    nn = D_MODEL // COL_BLOCK
    return pl.pallas_call(
        _merge_kernel,
        grid=(S // tm, nn),
        in_specs=[
            pl.BlockSpec((tm, NA_WIDTH), lambda i, n: (i, 0)),
            pl.BlockSpec((tm, ML_WIDTH), lambda i, n: (i, 0)),
            pl.BlockSpec((tm, ML_WIDTH), lambda i, n: (i, 0)),
            pl.BlockSpec((tm, COL_BLOCK), lambda i, n: (i, CB_ML_O)),
            pl.BlockSpec((tm, COL_BLOCK), lambda i, n: (i, CB_BR + n)),
            pl.BlockSpec((tm, COL_BLOCK), lambda i, n: (i, CB_BR + nn + n)),
            pl.BlockSpec((1, COL_BLOCK), lambda i, n: (0, n)),
            pl.BlockSpec((1, COL_BLOCK), lambda i, n: (0, nn + n)),
            pl.BlockSpec((1, ML_WIDTH), lambda i, n: (0, 0)),
            pl.BlockSpec((NA_WIDTH, COL_BLOCK), lambda i, n: (0, n)),
            pl.BlockSpec((ML_WIDTH, COL_BLOCK), lambda i, n: (0, n)),
        ],
        out_specs=pl.BlockSpec((tm, COL_BLOCK), lambda i, n: (i, n)),
        out_shape=jax.ShapeDtypeStruct((S, D_MODEL), BF16),
        scratch_shapes=[pltpu.VMEM((tm, ML_WIDTH), BF16)],
        compiler_params=_params(("parallel", "arbitrary")),
        name="merge",
    )(y_na, hf, hb, proj, proj, proj, b_branch, b_branch, hnorm_g, w_na_out, w_ml_out)


def _outproj_kernel(mg_ref, x_ref, wo_ref, g2_ref, wr_ref, br_ref, x1_ref, xn_ref, ids_ref, wts_ref):
    x1 = x_ref[...] + jnp.dot(mg_ref[...], wo_ref[...], preferred_element_type=F32)
    x1_ref[...] = x1
    ms = jnp.mean(x1 * x1, axis=-1, keepdims=True)
    xn = (x1 * lax.rsqrt(ms + EPS)) * g2_ref[...]
    xn_ref[...] = xn
    xh = xn.astype(BF16)
    xl = (xn - xh.astype(F32)).astype(BF16)
    hl = jnp.dot(xh, wr_ref[...], preferred_element_type=F32)
    logits = (hl[:, :LANES] + hl[:, LANES:]
              + jnp.dot(xl, wr_ref[:, :LANES], preferred_element_type=F32) + br_ref[...])
    tm = logits.shape[0]
    lane = lax.broadcasted_iota(jnp.int32, (tm, LANES), 1)
    big = jnp.int32(1 << 20)
    gmask = lane < N_GROUPS
    lg = jnp.where(gmask, logits, NEG_BIG)
    gmax = jnp.max(lg, axis=-1, keepdims=True)
    p_group = 1.0 / jnp.sum(jnp.exp(lg - gmax), axis=-1, keepdims=True)
    g_idx = jnp.min(jnp.where(lg == gmax, lane, big), axis=-1, keepdims=True)
    e_lo = N_GROUPS + g_idx * EXPERTS_PER_GROUP
    emask = (lane >= e_lo) & (lane < e_lo + EXPERTS_PER_GROUP)
    le = jnp.where(emask, logits, NEG_BIG)
    m1 = jnp.max(le, axis=-1, keepdims=True)
    i1 = jnp.min(jnp.where(le == m1, lane, big), axis=-1, keepdims=True)
    le2 = jnp.where(lane == i1, NEG_BIG, le)
    m2 = jnp.max(le2, axis=-1, keepdims=True)
    i2 = jnp.min(jnp.where(le2 == m2, lane, big), axis=-1, keepdims=True)
    e2 = jnp.exp(m2 - m1)
    w1 = p_group / (1.0 + e2)
    w2 = p_group * e2 / (1.0 + e2)
    ids_ref[...] = jnp.where(lane == 0, i1 - N_GROUPS, jnp.where(lane == 1, i2 - N_GROUPS, 0))
    wts_ref[...] = jnp.where(lane == 0, w1, jnp.where(lane == 1, w2, 0.0))


def _outproj(merged, x, w_o, g2, w_router, b_router, tm):
    S = x.shape[0]
    return pl.pallas_call(
        _outproj_kernel,
        grid=(S // tm,),
        in_specs=[
            pl.BlockSpec((tm, D_MODEL), lambda i: (i, 0)),
            pl.BlockSpec((tm, D_MODEL), lambda i: (i, 0)),
            pl.BlockSpec((D_MODEL, D_MODEL), lambda i: (0, 0), pipeline_mode=pl.Buffered(1)),
            pl.BlockSpec((1, D_MODEL), lambda i: (0, 0)),
            pl.BlockSpec((D_MODEL, 2 * LANES), lambda i: (0, 0)),
            pl.BlockSpec((1, LANES), lambda i: (0, 0)),
        ],
        out_specs=[
            pl.BlockSpec((tm, D_MODEL), lambda i: (i, 0)),
            pl.BlockSpec((tm, D_MODEL), lambda i: (i, 0)),
            pl.BlockSpec((tm, LANES), lambda i: (i, 0)),
            pl.BlockSpec((tm, LANES), lambda i: (i, 0)),
        ],
        out_shape=[
            jax.ShapeDtypeStruct((S, D_MODEL), F32),
            jax.ShapeDtypeStruct((S, D_MODEL), F32),
            jax.ShapeDtypeStruct((S, LANES), jnp.int32),
            jax.ShapeDtypeStruct((S, LANES), F32),
        ],
        compiler_params=_params(("parallel",)),
        name="outproj_router",
    )(merged, x, w_o, g2, w_router, b_router)


MOE_TM = 256


def _route_plan(ids, tm):
    n_tok = ids.shape[0]
    n_pairs = 2 * n_tok
    nt = n_pairs // tm + N_EXPERTS
    assert n_pairs & (n_pairs - 1) == 0 and nt * tm <= (1 << 16), "pair ids / rows are packed into 16-bit fields"
    shift = n_pairs.bit_length() - 1
    e = ids.T.reshape(-1)
    pid = jnp.arange(n_pairs, dtype=jnp.int32)
    skey = lax.sort(e * n_pairs + pid)
    order = skey & (n_pairs - 1)
    e_sorted = lax.shift_right_logical(skey, shift)
    cnt = jnp.sum((e[:, None] == jnp.arange(N_EXPERTS, dtype=jnp.int32)[None, :]).astype(jnp.int32), axis=0)
    tiles_e = (cnt + tm - 1) // tm
    tend = jnp.cumsum(tiles_e)
    tstart = tend - tiles_e
    cstart = jnp.cumsum(cnt) - cnt
    tile_idx = jnp.arange(nt, dtype=jnp.int32)
    te = jnp.minimum(jnp.sum((tile_idx[:, None] >= tend[None, :]).astype(jnp.int32), axis=1), N_EXPERTS - 1)
    local = (tile_idx - tstart[te]) * tm
    base = jnp.where(cnt[te] - local > 0, cstart[te] + local, 0)
    row_of_pos = tstart[e_sorted] * tm + (pid - cstart[e_sorted])
    dest = lax.sort(order.astype(jnp.uint32) * jnp.uint32(1 << 16) + row_of_pos.astype(jnp.uint32))
    dest = (dest & jnp.uint32(0xFFFF)).astype(jnp.int32)
    zero = jnp.zeros((1,), jnp.int32)
    ext = lambda a: jnp.concatenate([a.astype(jnp.int32), zero])
    return jnp.concatenate([order, pid[:tm]]), ext(te), ext(base), dest


def _moe_tile_kernel(te_ref, base_ref, order_ref, xn_hbm, wg0_ref, wu0_ref, wd0_ref, wg1_ref, wu1_ref, wd1_ref,
                     y_ref, gbuf0, gbuf1, gsem, *, tm, nt, n_tok):
    k = pl.program_id(0)
    gbuf = (gbuf0, gbuf1)
    wts = ((wg0_ref, wu0_ref, wd0_ref), (wg1_ref, wu1_ref, wd1_ref))

    def issue_gather(tile, par):
        first = base_ref[tile]
        for r in range(tm):
            tok = order_ref[first + r] & (n_tok - 1)
            pltpu.make_async_copy(xn_hbm.at[pl.ds(tok, 1), :], gbuf[par].at[pl.ds(r, 1), :], gsem.at[par]).start()

    def wait_gather(par):
        pltpu.make_async_copy(xn_hbm.at[pl.ds(0, tm), :], gbuf[par], gsem.at[par]).wait()

    def compute(par):
        wg_ref, wu_ref, wd_ref = wts[par]
        xg = gbuf[par][...].astype(BF16)
        hg = jnp.dot(xg, wg_ref[0], preferred_element_type=F32)
        hu = jnp.dot(xg, wu_ref[0], preferred_element_type=F32)
        hh = ((hg * jax.nn.sigmoid(hg)) * hu).astype(BF16)
        y_ref[par * tm:(par + 1) * tm, :] = jnp.dot(hh, wd_ref[0], preferred_element_type=F32)

    @pl.when(k == 0)
    def _():
        issue_gather(0, 0)

    wait_gather(0)
    issue_gather(2 * k + 1, 1)
    compute(0)
    wait_gather(1)
    issue_gather(2 * k + 2, 0)
    compute(1)

    @pl.when(k == nt // 2 - 1)
    def _():
        wait_gather(0)


def _moe_tiles(xn2, order, tile_expert, tile_base, w_gate, w_up, w_down, tm):
    n_tok = xn2.shape[0]
    assert n_tok & (n_tok - 1) == 0, "token count must be a power of two (pair id -> token by masking)"
    nt = tile_expert.shape[0] - 1
    assert nt % 2 == 0

    def wspec(shape, par):
        return pl.BlockSpec(shape, lambda k, te, base, order: (te[2 * k + par], 0, 0))

    w_specs = [wspec(s, par) for par in (0, 1)
               for s in ((1, D_MODEL, D_EXPERT), (1, D_MODEL, D_EXPERT), (1, D_EXPERT, D_MODEL))]
    return pl.pallas_call(
        functools.partial(_moe_tile_kernel, tm=tm, nt=nt, n_tok=n_tok),
        grid_spec=pltpu.PrefetchScalarGridSpec(
            num_scalar_prefetch=3,
            grid=(nt // 2,),
            in_specs=[pl.BlockSpec(memory_space=pl.ANY)] + w_specs,
            out_specs=pl.BlockSpec((2 * tm, D_MODEL), lambda k, te, base, order: (k, 0)),
            scratch_shapes=[
                pltpu.VMEM((tm, D_MODEL), F32),
                pltpu.VMEM((tm, D_MODEL), F32),
                pltpu.SemaphoreType.DMA((2,)),
            ],
        ),
        out_shape=jax.ShapeDtypeStruct((nt * tm, D_MODEL), F32),
        compiler_params=_params(("arbitrary",), disable_bounds_checks=True),
        name="moe_tiles",
    )(tile_expert, tile_base, order, xn2, w_gate, w_up, w_down, w_gate, w_up, w_down)


def _combine_kernel(dest_ref, x1_ref, wts_ref, ys_hbm, out_ref, ybuf, sem, *, tm, nblk, n_tok):
    i = pl.program_id(0)
    slot = i % 2

    def issue(blk, dst_slot):
        t0 = blk * tm
        for r in range(tm):
            for j in range(2):
                d = dest_ref[j * n_tok + t0 + r]
                pltpu.make_async_copy(ys_hbm.at[pl.ds(d, 1), :], ybuf.at[dst_slot, j, pl.ds(r, 1), :],
                                      sem.at[dst_slot]).start()

    def wait(w_slot):
        for j in range(2):
            pltpu.make_async_copy(ys_hbm.at[pl.ds(0, tm), :], ybuf.at[w_slot, j], sem.at[w_slot]).wait()

    @pl.when(i == 0)
    def _():
        issue(0, 0)

    issue(jnp.minimum(i + 1, nblk - 1), 1 - slot)
    wait(slot)
    w = wts_ref[...]
    out_ref[...] = x1_ref[...] + (w[:, 0:1] * ybuf[slot, 0] + w[:, 1:2] * ybuf[slot, 1])

    @pl.when(i == nblk - 1)
    def _():
        wait(1 - slot)


def _combine(x1, ys, dest, wts, tm):
    S = x1.shape[0]
    nblk = S // tm
    return pl.pallas_call(
        functools.partial(_combine_kernel, tm=tm, nblk=nblk, n_tok=S),
        grid_spec=pltpu.PrefetchScalarGridSpec(
            num_scalar_prefetch=1,
            grid=(nblk,),
            in_specs=[
                pl.BlockSpec((tm, D_MODEL), lambda i, dest: (i, 0)),
                pl.BlockSpec((tm, LANES), lambda i, dest: (i, 0)),
                pl.BlockSpec(memory_space=pl.ANY),
            ],
            out_specs=pl.BlockSpec((tm, D_MODEL), lambda i, dest: (i, 0)),
            scratch_shapes=[
                pltpu.VMEM((2, 2, tm, D_MODEL), F32),
                pltpu.SemaphoreType.DMA((2,)),
            ],
        ),
        out_shape=jax.ShapeDtypeStruct((S, D_MODEL), F32),
        compiler_params=_params(("arbitrary",), disable_bounds_checks=True),
        name="moe_combine",
    )(dest, x1, wts, ys)


def _moe(xn2, x1, ids, wts, w_gate, w_up, w_down):
    tm = MOE_TM
    order, tile_expert, tile_base, dest = _route_plan(ids[:, :2], tm)
    ys = _moe_tiles(xn2, order, tile_expert, tile_base, w_gate, w_up, w_down, tm)
    return _combine(x1, ys, dest, wts, _tile(x1.shape[0], MOE_TM))


def _tile(S, want):
    return min(want, S)


def _layer(x, p):
    S = x.shape[0]
    proj, gates = _inproj(x, p["g1"], p["w_main"], p["w_gate_in"], p["b_gate_in"], p["qk_gain"], _tile(S, 512))
    y_na = _na_attention(proj, p["na_bias"])
    qk = _conv_silu(proj, p["conv_w"], p["conv_b"], _tile(S, 512))
    G = 8
    gates_t = jnp.transpose(gates[:, :N_ML_GATES].reshape(S // ML_CHUNK, ML_CHUNK, N_ML_GATES), (0, 2, 1))
    hf, hb = _mlstm_scan(qk, proj, gates, gates_t, G)
    merged = _merge(y_na, hf, hb, proj, p["b_branch"], p["hnorm_g"], p["w_na_out"], p["w_ml_out"], _tile(S, 512))
    x1, xn2, ids, wts = _outproj(merged, x, p["w_o"], p["g2"], p["w_router"], p["b_router"], _tile(S, 512))
    return _moe(xn2, x1, ids, wts, p["w_gate"], p["w_up"], p["w_down"])


def _prepare(l, norm1_g, w_in, b_ml_gates, b_branch, qn_g, kn_g, na_rpb, ml_conv_w, ml_conv_b, ml_hnorm_g,
             w_na_out, w_ml_out, w_o, norm2_g, w_router_group, b_router_group, w_router_expert,
             b_router_expert, w_gate, w_up, w_down):
    g_lo = 3 * NA_WIDTH + 4 * ML_WIDTH
    g_hi = g_lo + N_ML_GATES
    w = w_in[l]
    pad = LANES - N_ML_GATES
    n_r = N_GROUPS + N_EXPERTS
    w_r = jnp.pad(jnp.concatenate([w_router_group[l], w_router_expert[l]], axis=1), ((0, 0), (0, LANES - n_r)))
    w_r_hi = w_r.astype(BF16)
    w_r_lo = (w_r - w_r_hi.astype(F32)).astype(BF16)
    return {
        "g1": norm1_g[l].reshape(1, D_MODEL),
        "w_main": jnp.concatenate([w[:, :g_lo], w[:, g_hi:]], axis=1).astype(BF16),
        "w_gate_in": jnp.pad(w[:, g_lo:g_hi], ((0, 0), (0, pad))).astype(BF16),
        "b_gate_in": jnp.pad(b_ml_gates[l], (0, pad)).reshape(1, LANES),
        "qk_gain": jnp.concatenate([qn_g[l].reshape(-1), kn_g[l].reshape(-1)]).reshape(1, 2 * NA_WIDTH),
        "na_bias": _na_bias_table(na_rpb[l]),
        "conv_w": ml_conv_w[l],
        "conv_b": ml_conv_b[l].reshape(1, 2 * ML_WIDTH),
        "b_branch": b_branch[l].reshape(1, 2 * D_MODEL),
        "hnorm_g": ml_hnorm_g[l].reshape(1, ML_WIDTH),
        "w_na_out": w_na_out[l].astype(BF16),
        "w_ml_out": w_ml_out[l].astype(BF16),
        "w_o": w_o[l].astype(BF16),
        "g2": norm2_g[l].reshape(1, D_MODEL),
        "w_router": jnp.concatenate([w_r_hi, w_r_lo], axis=1),
        "b_router": jnp.pad(jnp.concatenate([b_router_group[l], b_router_expert[l]]), (0, LANES - n_r)).reshape(1, LANES),
        "w_gate": w_gate[l].astype(BF16),
        "w_up": w_up[l].astype(BF16),
        "w_down": w_down[l].astype(BF16),
    }


def kernel(x_prompt, x_sample, norm1_g, w_in, b_ml_gates, b_branch, qn_g, kn_g, na_rpb, ml_conv_w, ml_conv_b,
           ml_hnorm_g, w_na_out, w_ml_out, w_o, norm2_g, w_router_group, b_router_group, w_router_expert,
           b_router_expert, w_gate, w_up, w_down):
    depth = w_in.shape[0]
    layers = [_prepare(l, norm1_g, w_in, b_ml_gates, b_branch, qn_g, kn_g, na_rpb, ml_conv_w, ml_conv_b,
                       ml_hnorm_g, w_na_out, w_ml_out, w_o, norm2_g, w_router_group, b_router_group,
                       w_router_expert, b_router_expert, w_gate, w_up, w_down) for l in range(depth)]

    def trunk(x):
        b, s, d = x.shape
        outs = []
        for bi in range(b):
            h = x[bi]
            for p in layers:
                h = _layer(h, p)
            outs.append(h)
        return jnp.stack(outs)

    return (trunk(x_prompt), trunk(x_sample))
```

```python
import functools

import numpy as np
import jax
import jax.numpy as jnp
from jax import lax
from jax.experimental import pallas as pl
from jax.experimental.pallas import tpu as pltpu

F32 = jnp.float32
BF16 = jnp.bfloat16

D_MODEL = 2048
GRID_W = 64
NA_HEADS = 8
NA_HEAD_DIM = 128
NA_WIDTH = NA_HEADS * NA_HEAD_DIM
NA_ROWS = 8
NA_COLS = 16
ML_HEADS = 4
ML_HEAD_DIM = 256
ML_WIDTH = ML_HEADS * ML_HEAD_DIM
ML_CHUNK = 64
N_ML_GATES = 4 * ML_HEADS
N_GROUPS = 4
EXPERTS_PER_GROUP = 4
N_EXPERTS = N_GROUPS * EXPERTS_PER_GROUP
D_EXPERT = 512
EPS = 1e-6

LANES = 128
COL_BLOCK = 1024
N_MAIN = 3 * NA_WIDTH + 4 * ML_WIDTH + 2 * D_MODEL
CB_NA_Q, CB_NA_K, CB_NA_V, CB_ML_Q, CB_ML_K, CB_ML_V, CB_ML_O, CB_BR = 0, 1, 2, 3, 4, 5, 6, 7
NEG_BIG = -1e30
VMEM_LIMIT = 56 * 1024 * 1024


def _sigmoid(x):
    return 0.5 * jnp.tanh(0.5 * x) + 0.5


def _params(sem, vmem=VMEM_LIMIT, **kw):
    return pltpu.CompilerParams(dimension_semantics=sem, vmem_limit_bytes=vmem, **kw)


def _inproj_kernel(x_ref, g1_ref, w_ref, wg_ref, bg_ref, qkg_ref, proj_ref, gates_ref, xn_ref):
    j = pl.program_id(1)

    @pl.when(j == 0)
    def _():
        x = x_ref[...]
        ms = jnp.mean(x * x, axis=-1, keepdims=True)
        xn = ((x * lax.rsqrt(ms + EPS)) * g1_ref[...]).astype(BF16)
        xn_ref[...] = xn
        gates_ref[...] = jnp.dot(xn, wg_ref[...], preferred_element_type=F32) + bg_ref[...]

    acc = jnp.dot(xn_ref[...], w_ref[...], preferred_element_type=F32)

    @pl.when(j <= CB_NA_K)
    def _():
        for h in range(NA_HEADS):
            sl = slice(h * NA_HEAD_DIM, (h + 1) * NA_HEAD_DIM)
            a = acc[:, sl]
            ms = jnp.mean(a * a, axis=-1, keepdims=True)
            proj_ref[:, sl] = ((a * lax.rsqrt(ms + EPS)) * qkg_ref[:, sl]).astype(BF16)

    @pl.when(j > CB_NA_K)
    def _():
        proj_ref[...] = acc.astype(BF16)


def _inproj(x, g1, w_main, w_gate, b_gate, qk_gain, tm):
    S = x.shape[0]
    nj = N_MAIN // COL_BLOCK
    return pl.pallas_call(
        _inproj_kernel,
        grid=(S // tm, nj),
        in_specs=[
            pl.BlockSpec((tm, D_MODEL), lambda i, j: (i, 0)),
            pl.BlockSpec((1, D_MODEL), lambda i, j: (0, 0)),
            pl.BlockSpec((D_MODEL, COL_BLOCK), lambda i, j: (0, j)),
            pl.BlockSpec((D_MODEL, LANES), lambda i, j: (0, 0)),
            pl.BlockSpec((1, LANES), lambda i, j: (0, 0)),
            pl.BlockSpec((1, COL_BLOCK), lambda i, j: (0, jnp.minimum(j, CB_NA_K))),
        ],
        out_specs=[
            pl.BlockSpec((tm, COL_BLOCK), lambda i, j: (i, j)),
            pl.BlockSpec((tm, LANES), lambda i, j: (i, 0)),
        ],
        out_shape=[
            jax.ShapeDtypeStruct((S, N_MAIN), BF16),
            jax.ShapeDtypeStruct((S, LANES), F32),
        ],
        scratch_shapes=[pltpu.VMEM((tm, D_MODEL), BF16)],
        compiler_params=_params(("parallel", "arbitrary")),
        name="inproj",
    )(x, g1, w_main, w_gate, b_gate, qk_gain)


NA_QROWS = 8
NA_QTOK = NA_QROWS * GRID_W
NA_KTOK = NA_ROWS * GRID_W


def _na_bias_table(rpb):
    c = np.arange(GRID_W)
    start_c = np.clip(c - NA_COLS // 2, 0, GRID_W - NA_COLS)
    kc = np.arange(GRID_W)
    mask = (kc[None, :] >= start_c[:, None]) & (kc[None, :] < start_c[:, None] + NA_COLS)
    dc = np.clip(kc[None, :] - c[:, None] + NA_COLS - 1, 0, 2 * NA_COLS - 2)
    e = jnp.where(mask[None, None], rpb.astype(F32)[:, :, dc], NEG_BIG)
    dr = np.arange(NA_ROWS)[None, :] - np.arange(NA_ROWS)[:, None] + NA_ROWS - 1
    b = e[:, dr]
    b = jnp.transpose(b, (1, 0, 3, 2, 4))
    return b.reshape(NA_ROWS * NA_HEADS, GRID_W, NA_KTOK)


def _na_kernel(q_ref, kp_ref, kc_ref, kn_ref, vp_ref, vc_ref, vn_ref, bias_ref, o_ref, kw_ref, vw_ref,
               s_ref, p_ref, l_ref, *, rows):
    b = pl.program_id(0)
    kw_ref[0:NA_QTOK] = kp_ref[...]
    kw_ref[NA_QTOK:2 * NA_QTOK] = kc_ref[...]
    kw_ref[2 * NA_QTOK:3 * NA_QTOK] = kn_ref[...]
    vw_ref[0:NA_QTOK] = vp_ref[...]
    vw_ref[NA_QTOK:2 * NA_QTOK] = vc_ref[...]
    vw_ref[2 * NA_QTOK:3 * NA_QTOK] = vn_ref[...]
    scale = NA_HEAD_DIM ** -0.5

    def row_body(rl, carry):
        r = b * NA_QROWS + rl
        rs = jnp.clip(r - NA_ROWS // 2, 0, rows - NA_ROWS)
        d = r - rs
        koff = pl.multiple_of((rs - (b - 1) * NA_QROWS) * GRID_W, GRID_W)
        qoff = pl.multiple_of(rl * GRID_W, GRID_W)
        heads = [slice(h * NA_HEAD_DIM, (h + 1) * NA_HEAD_DIM) for h in range(NA_HEADS)]
        for h, sl in enumerate(heads):
            q = q_ref[pl.ds(qoff, GRID_W), sl]
            k = kw_ref[pl.ds(koff, NA_KTOK), sl]
            s = lax.dot_general(q, k, (((1,), (1,)), ((), ())), preferred_element_type=F32)
            s_ref[h] = s * scale + bias_ref[d * NA_HEADS + h]
        for h in range(NA_HEADS):
            s = s_ref[h]
            p = jnp.exp(s - jnp.max(s, axis=-1, keepdims=True))
            l_ref[h] = jnp.sum(p, axis=-1, keepdims=True)
            p_ref[h] = p.astype(BF16)
        for h, sl in enumerate(heads):
            v = vw_ref[pl.ds(koff, NA_KTOK), sl]
            o = jnp.dot(p_ref[h], v, preferred_element_type=F32)
            o_ref[pl.ds(qoff, GRID_W), sl] = (o / l_ref[h]).astype(BF16)
        return carry

    lax.fori_loop(0, NA_QROWS, row_body, 0)


def _na_attention(proj, bias_tab):
    S = proj.shape[0]
    rows = S // GRID_W
    nb = rows // NA_QROWS
    blk = (NA_QTOK, COL_BLOCK)
    prev = lambda b: jnp.maximum(b - 1, 0)
    nxt = lambda b: jnp.minimum(b + 1, nb - 1)
    return pl.pallas_call(
        functools.partial(_na_kernel, rows=rows),
        grid=(nb,),
        in_specs=[
            pl.BlockSpec(blk, lambda b: (b, CB_NA_Q)),
            pl.BlockSpec(blk, lambda b: (prev(b), CB_NA_K)),
            pl.BlockSpec(blk, lambda b: (b, CB_NA_K)),
            pl.BlockSpec(blk, lambda b: (nxt(b), CB_NA_K)),
            pl.BlockSpec(blk, lambda b: (prev(b), CB_NA_V)),
            pl.BlockSpec(blk, lambda b: (b, CB_NA_V)),
            pl.BlockSpec(blk, lambda b: (nxt(b), CB_NA_V)),
            pl.BlockSpec((NA_ROWS * NA_HEADS, GRID_W, NA_KTOK), lambda b: (0, 0, 0)),
        ],
        out_specs=pl.BlockSpec(blk, lambda b: (b, 0)),
        out_shape=jax.ShapeDtypeStruct((S, NA_WIDTH), BF16),
        scratch_shapes=[
            pltpu.VMEM((3 * NA_QTOK, NA_WIDTH), BF16),
            pltpu.VMEM((3 * NA_QTOK, NA_WIDTH), BF16),
            pltpu.VMEM((NA_HEADS, GRID_W, NA_KTOK), F32),
            pltpu.VMEM((NA_HEADS, GRID_W, NA_KTOK), BF16),
            pltpu.VMEM((NA_HEADS, GRID_W, 1), F32),
        ],
        compiler_params=_params(("parallel",)),
        name="na_attention",
    )(proj, proj, proj, proj, proj, proj, proj, bias_tab)


HALO = 16


ML_KCHUNK = 256


def _conv_kernel(x_ref, xp_ref, xn_ref, w_ref, b_ref, o_ref, *, tb, nblk, is_key):
    i = pl.program_id(0)
    x = x_ref[...].astype(F32)
    prev_row = jnp.where(i > 0, xp_ref[HALO - 1:HALO, :].astype(F32), 0.0)
    next_row = jnp.where(i < nblk - 1, xn_ref[0:1, :].astype(F32), 0.0)
    ridx = lax.broadcasted_iota(jnp.int32, (tb, 1), 0)
    x_prev = jnp.where(ridx == 0, prev_row, pltpu.roll(x, 1, 0))
    x_next = jnp.where(ridx == tb - 1, next_row, pltpu.roll(x, tb - 1, 0))
    y = b_ref[...] + x_prev * w_ref[0:1, :] + x * w_ref[1:2, :] + x_next * w_ref[2:3, :]
    y = y * _sigmoid(y)
    if is_key:
        y = y * (ML_HEAD_DIM ** -0.5)
        for j in range(tb // ML_KCHUNK):
            o_ref[j] = y[j * ML_KCHUNK:(j + 1) * ML_KCHUNK, :].T.astype(BF16)
    else:
        o_ref[...] = y.astype(BF16)


def _conv_silu(proj, conv_w, conv_b, tb, is_key):
    S = proj.shape[0]
    nblk = S // tb
    hb = tb // HALO
    c = 1 if is_key else 0
    if is_key:
        out_spec = pl.BlockSpec((tb // ML_KCHUNK, ML_WIDTH, ML_KCHUNK), lambda i: (i, 0, 0))
        out_shape = jax.ShapeDtypeStruct((S // ML_KCHUNK, ML_WIDTH, ML_KCHUNK), BF16)
    else:
        out_spec = pl.BlockSpec((tb, ML_WIDTH), lambda i: (i, 0))
        out_shape = jax.ShapeDtypeStruct((S, ML_WIDTH), BF16)
    return pl.pallas_call(
        functools.partial(_conv_kernel, tb=tb, nblk=nblk, is_key=is_key),
        grid=(nblk,),
        in_specs=[
            pl.BlockSpec((tb, COL_BLOCK), lambda i: (i, CB_ML_Q + c)),
            pl.BlockSpec((HALO, COL_BLOCK), lambda i: (jnp.maximum(i * hb - 1, 0), CB_ML_Q + c)),
            pl.BlockSpec((HALO, COL_BLOCK), lambda i: (jnp.minimum((i + 1) * hb, S // HALO - 1), CB_ML_Q + c)),
            pl.BlockSpec((3, COL_BLOCK), lambda i: (0, c)),
            pl.BlockSpec((1, COL_BLOCK), lambda i: (0, c)),
        ],
        out_specs=out_spec,
        out_shape=out_shape,
        compiler_params=_params(("parallel",)),
        name="conv_silu_k" if is_key else "conv_silu_q",
    )(proj, proj, proj, conv_w, conv_b)


ML_AUG = ML_HEAD_DIM + LANES


def _log_sigmoid(x):
    return jnp.minimum(x, 0.0) - jnp.log1p(jnp.exp(-jnp.abs(x)))


def _split3(x):
    x1 = x.astype(BF16)
    r1 = x - x1.astype(F32)
    x2 = r1.astype(BF16)
    x3 = (r1 - x2.astype(F32)).astype(BF16)
    return x1, x2, x3


def _mlstm_kernel(qf_ref, ktf_ref, vf_ref, grf_ref, qb_ref, ktb_ref, vb_ref, grb_ref,
                  hf_ref, hb_ref, c_ref, m_ref, *, G):
    @pl.when(pl.program_id(0) == 0)
    def _():
        c_ref[...] = jnp.zeros_like(c_ref)
        m_ref[...] = jnp.zeros_like(m_ref)

    L = ML_KCHUNK
    ri = lax.broadcasted_iota(jnp.int32, (L, L), 0)
    ci = lax.broadcasted_iota(jnp.int32, (L, L), 1)
    tril = ci <= ri
    triu = ci >= ri
    trilb = tril.astype(BF16)
    triub = triu.astype(BF16)
    ones_blk = jnp.ones((L, LANES), BF16)
    pad_rows = jnp.zeros((LANES - N_ML_GATES, L), F32)

    def chain(dirn, cc):
        if dirn == 0:
            q_ref, kt_ref, v_ref, gr_ref, h_ref = qf_ref, ktf_ref, vf_ref, grf_ref, hf_ref
            tri_m, mask = triub, tril
        else:
            q_ref, kt_ref, v_ref, gr_ref, h_ref = qb_ref, ktb_ref, vb_ref, grb_ref, hb_ref
            tri_m, mask = trilb, triu
        ioff = dirn * 2 * ML_HEADS
        foff = ioff + ML_HEADS
        toff = pl.multiple_of(cc * L, L)
        grow = gr_ref[cc]
        br = jnp.dot(jnp.concatenate(_split3(_log_sigmoid(grow)), axis=0), tri_m, preferred_element_type=F32)
        brow_all = br[:N_ML_GATES] + br[N_ML_GATES:2 * N_ML_GATES] + br[2 * N_ML_GATES:]
        bcol_all = jnp.concatenate([brow_all, pad_rows], axis=0).T
        for h in range(ML_HEADS):
            ch = dirn * ML_HEADS + h
            m = m_ref[ch:ch + 1, 0:1]
            bcol = bcol_all[:, foff + h:foff + h + 1]
            brow = brow_all[foff + h:foff + h + 1, :]
            irow = grow[ioff + h:ioff + h + 1, :]
            log_d = jnp.where(mask, bcol - brow + irow, NEG_BIG)
            m_t = jnp.maximum(bcol + m, jnp.max(log_d, axis=-1, keepdims=True))
            dmat = jnp.exp(log_d - m_t)
            inter = jnp.exp(bcol + m - m_t)
            hs = slice(h * ML_HEAD_DIM, (h + 1) * ML_HEAD_DIM)
            q = q_ref[pl.ds(toff, L), hs]
            kt = kt_ref[cc, hs, :]
            v = v_ref[pl.ds(toff, L), hs]
            vaug = jnp.concatenate([v, ones_blk], axis=1)
            s = jnp.dot(q, kt, preferred_element_type=F32)
            sd = (s * dmat).astype(BF16)
            cst = c_ref[ch]
            num = (jnp.dot(sd, vaug, preferred_element_type=F32)
                   + inter * jnp.dot(q, cst.astype(BF16), preferred_element_type=F32))
            den = num[:, ML_HEAD_DIM:ML_HEAD_DIM + 1]
            h_ref[pl.ds(toff, L), hs] = num[:, :ML_HEAD_DIM] / jnp.maximum(jnp.abs(den), jnp.exp(-m_t))
            g = brow[:, L - 1:L] if dirn == 0 else brow[:, 0:1]
            log_w = g - brow + irow
            m_new = jnp.maximum(g + m, jnp.max(log_w, axis=-1, keepdims=True))
            w = jnp.exp(log_w - m_new)
            decay = jnp.exp(g + m - m_new)
            kw = (kt.astype(F32) * w).astype(BF16)
            upd = jnp.dot(kw, vaug, preferred_element_type=F32)
            c_ref[ch] = decay * cst + upd
            m_ref[ch:ch + 1, :] = jnp.broadcast_to(m_new, (1, LANES))

    def chunk_body(c, carry):
        chain(0, c)
        chain(1, G - 1 - c)
        return carry

    lax.fori_loop(0, G, chunk_body, 0)


def _mlstm_scan(q, kt, proj, gates_t, G):
    S = q.shape[0]
    tb = G * ML_KCHUNK
    nb = S // tb
    fwd = lambda i: i
    bwd = lambda i: nb - 1 - i
    in_specs = []
    for mp in (fwd, bwd):
        in_specs += [
            pl.BlockSpec((tb, ML_WIDTH), lambda i, mp=mp: (mp(i), 0)),
            pl.BlockSpec((G, ML_WIDTH, ML_KCHUNK), lambda i, mp=mp: (mp(i), 0, 0)),
            pl.BlockSpec((tb, COL_BLOCK), lambda i, mp=mp: (mp(i), CB_ML_V)),
            pl.BlockSpec((G, N_ML_GATES, ML_KCHUNK), lambda i, mp=mp: (mp(i), 0, 0)),
        ]
    return pl.pallas_call(
        functools.partial(_mlstm_kernel, G=G),
        grid=(nb,),
        in_specs=in_specs,
        out_specs=[
            pl.BlockSpec((tb, ML_WIDTH), lambda i: (fwd(i), 0)),
            pl.BlockSpec((tb, ML_WIDTH), lambda i: (bwd(i), 0)),
        ],
        out_shape=[jax.ShapeDtypeStruct((S, ML_WIDTH), F32), jax.ShapeDtypeStruct((S, ML_WIDTH), F32)],
        scratch_shapes=[
            pltpu.VMEM((2 * ML_HEADS, ML_HEAD_DIM, ML_AUG), F32),
            pltpu.VMEM((2 * ML_HEADS, LANES), F32),
        ],
        compiler_params=_params(("arbitrary",)),
        name="mlstm_scan",
    )(q, kt, proj, gates_t, q, kt, proj, gates_t)


def _merge_kernel(yna_ref, hf_ref, hb_ref, o_ref, ga_ref, gb_ref, ba_ref, bb_ref, hg_ref, wna_ref, wml_ref,
                  out_ref, yml_ref):
    n = pl.program_id(1)

    @pl.when(n == 0)
    def _():
        for h in range(ML_HEADS):
            sl = slice(h * ML_HEAD_DIM, (h + 1) * ML_HEAD_DIM)
            hh = hf_ref[:, sl] + hb_ref[:, sl]
            ms = jnp.mean(hh * hh, axis=-1, keepdims=True)
            y = (hh * lax.rsqrt(ms + EPS)) * hg_ref[:, sl]
            yml_ref[:, sl] = (y * _sigmoid(o_ref[:, sl].astype(F32))).astype(BF16)

    ga = _sigmoid(ga_ref[...].astype(F32) + ba_ref[...])
    gb = _sigmoid(gb_ref[...].astype(F32) + bb_ref[...])
    a = jnp.dot(yna_ref[...], wna_ref[...], preferred_element_type=F32)
    bm = jnp.dot(yml_ref[...], wml_ref[...], preferred_element_type=F32)
    out_ref[...] = (ga * a + gb * bm).astype(BF16)


def _merge(y_na, hf, hb, proj, b_branch, hnorm_g, w_na_out, w_ml_out, tm):
    S = y_na.shape[0]
    nn = D_MODEL // COL_BLOCK
    return pl.pallas_call(
        _merge_kernel,
        grid=(S // tm, nn),
        in_specs=[
            pl.BlockSpec((tm, NA_WIDTH), lambda i, n: (i, 0)),
            pl.BlockSpec((tm, ML_WIDTH), lambda i, n: (i, 0)),
            pl.BlockSpec((tm, ML_WIDTH), lambda i, n: (i, 0)),
            pl.BlockSpec((tm, COL_BLOCK), lambda i, n: (i, CB_ML_O)),
            pl.BlockSpec((tm, COL_BLOCK), lambda i, n: (i, CB_BR + n)),
            pl.BlockSpec((tm, COL_BLOCK), lambda i, n: (i, CB_BR + nn + n)),
            pl.BlockSpec((1, COL_BLOCK), lambda i, n: (0, n)),
            pl.BlockSpec((1, COL_BLOCK), lambda i, n: (0, nn + n)),
            pl.BlockSpec((1, ML_WIDTH), lambda i, n: (0, 0)),
            pl.BlockSpec((NA_WIDTH, COL_BLOCK), lambda i, n: (0, n)),
            pl.BlockSpec((ML_WIDTH, COL_BLOCK), lambda i, n: (0, n)),
        ],
        out_specs=pl.BlockSpec((tm, COL_BLOCK), lambda i, n: (i, n)),
        out_shape=jax.ShapeDtypeStruct((S, D_MODEL), BF16),
        scratch_shapes=[pltpu.VMEM((tm, ML_WIDTH), BF16)],
        compiler_params=_params(("parallel", "arbitrary")),
        name="merge",
    )(y_na, hf, hb, proj, proj, proj, b_branch, b_branch, hnorm_g, w_na_out, w_ml_out)


def _outproj_kernel(mg_ref, x_ref, wo_ref, g2_ref, wr_ref, br_ref, x1_ref, xn_ref, ids_ref, wts_ref):
    x1 = x_ref[...] + jnp.dot(mg_ref[...], wo_ref[...], preferred_element_type=F32)
    x1_ref[...] = x1
    ms = jnp.mean(x1 * x1, axis=-1, keepdims=True)
    xn = (x1 * lax.rsqrt(ms + EPS)) * g2_ref[...]
    xn_ref[...] = xn
    xh = xn.astype(BF16)
    xl = (xn - xh.astype(F32)).astype(BF16)
    hl = jnp.dot(xh, wr_ref[...], preferred_element_type=F32)
    logits = (hl[:, :LANES] + hl[:, LANES:]
              + jnp.dot(xl, wr_ref[:, :LANES], preferred_element_type=F32) + br_ref[...])
    tm = logits.shape[0]
    lane = lax.broadcasted_iota(jnp.int32, (tm, LANES), 1)
    big = jnp.int32(1 << 20)
    gmask = lane < N_GROUPS
    lg = jnp.where(gmask, logits, NEG_BIG)
    gmax = jnp.max(lg, axis=-1, keepdims=True)
    p_group = 1.0 / jnp.sum(jnp.exp(lg - gmax), axis=-1, keepdims=True)
    g_idx = jnp.min(jnp.where(lg == gmax, lane, big), axis=-1, keepdims=True)
    e_lo = N_GROUPS + g_idx * EXPERTS_PER_GROUP
    emask = (lane >= e_lo) & (lane < e_lo + EXPERTS_PER_GROUP)
    le = jnp.where(emask, logits, NEG_BIG)
    m1 = jnp.max(le, axis=-1, keepdims=True)
    i1 = jnp.min(jnp.where(le == m1, lane, big), axis=-1, keepdims=True)
    le2 = jnp.where(lane == i1, NEG_BIG, le)
    m2 = jnp.max(le2, axis=-1, keepdims=True)
    i2 = jnp.min(jnp.where(le2 == m2, lane, big), axis=-1, keepdims=True)
    e2 = jnp.exp(m2 - m1)
    w1 = p_group / (1.0 + e2)
    w2 = p_group * e2 / (1.0 + e2)
    ids_ref[...] = jnp.where(lane == 0, i1 - N_GROUPS, jnp.where(lane == 1, i2 - N_GROUPS, 0))
    wts_ref[...] = jnp.where(lane == 0, w1, jnp.where(lane == 1, w2, 0.0))


def _outproj(merged, x, w_o, g2, w_router, b_router, tm):
    S = x.shape[0]
    return pl.pallas_call(
        _outproj_kernel,
        grid=(S // tm,),
        in_specs=[
            pl.BlockSpec((tm, D_MODEL), lambda i: (i, 0)),
            pl.BlockSpec((tm, D_MODEL), lambda i: (i, 0)),
            pl.BlockSpec((D_MODEL, D_MODEL), lambda i: (0, 0), pipeline_mode=pl.Buffered(1)),
            pl.BlockSpec((1, D_MODEL), lambda i: (0, 0)),
            pl.BlockSpec((D_MODEL, 2 * LANES), lambda i: (0, 0)),
            pl.BlockSpec((1, LANES), lambda i: (0, 0)),
        ],
        out_specs=[
            pl.BlockSpec((tm, D_MODEL), lambda i: (i, 0)),
            pl.BlockSpec((tm, D_MODEL), lambda i: (i, 0)),
            pl.BlockSpec((tm, LANES), lambda i: (i, 0)),
            pl.BlockSpec((tm, LANES), lambda i: (i, 0)),
        ],
        out_shape=[
            jax.ShapeDtypeStruct((S, D_MODEL), F32),
            jax.ShapeDtypeStruct((S, D_MODEL), F32),
            jax.ShapeDtypeStruct((S, LANES), jnp.int32),
            jax.ShapeDtypeStruct((S, LANES), F32),
        ],
        compiler_params=_params(("parallel",)),
        name="outproj_router",
    )(merged, x, w_o, g2, w_router, b_router)


MOE_TM = 256


def _route_plan(ids, tm):
    n_tok = ids.shape[0]
    n_pairs = 2 * n_tok
    nt = n_pairs // tm + N_EXPERTS
    assert n_pairs & (n_pairs - 1) == 0 and nt * tm <= (1 << 16), "pair ids / rows are packed into 16-bit fields"
    shift = n_pairs.bit_length() - 1
    e = ids.T.reshape(-1)
    pid = jnp.arange(n_pairs, dtype=jnp.int32)
    skey = lax.sort(e * n_pairs + pid)
    order = skey & (n_pairs - 1)
    e_sorted = lax.shift_right_logical(skey, shift)
    cnt = jnp.sum((e[:, None] == jnp.arange(N_EXPERTS, dtype=jnp.int32)[None, :]).astype(jnp.int32), axis=0)
    tiles_e = (cnt + tm - 1) // tm
    tend = jnp.cumsum(tiles_e)
    tstart = tend - tiles_e
    cstart = jnp.cumsum(cnt) - cnt
    tile_idx = jnp.arange(nt, dtype=jnp.int32)
    te = jnp.minimum(jnp.sum((tile_idx[:, None] >= tend[None, :]).astype(jnp.int32), axis=1), N_EXPERTS - 1)
    local = (tile_idx - tstart[te]) * tm
    base = jnp.where(cnt[te] - local > 0, cstart[te] + local, 0)
    row_of_pos = tstart[e_sorted] * tm + (pid - cstart[e_sorted])
    dest = lax.sort(order.astype(jnp.uint32) * jnp.uint32(1 << 16) + row_of_pos.astype(jnp.uint32))
    dest = (dest & jnp.uint32(0xFFFF)).astype(jnp.int32)
    zero = jnp.zeros((1,), jnp.int32)
    ext = lambda a: jnp.concatenate([a.astype(jnp.int32), zero])
    return jnp.concatenate([order, pid[:tm]]), ext(te), ext(base), dest


def _moe_tile_kernel(te_ref, base_ref, order_ref, xn_hbm, wg0_ref, wu0_ref, wd0_ref, wg1_ref, wu1_ref, wd1_ref,
                     y_ref, gbuf0, gbuf1, gsem, *, tm, nt, n_tok):
    k = pl.program_id(0)
    gbuf = (gbuf0, gbuf1)
    wts = ((wg0_ref, wu0_ref, wd0_ref), (wg1_ref, wu1_ref, wd1_ref))

    def issue_gather(tile, par):
        first = base_ref[tile]
        for r in range(tm):
            tok = order_ref[first + r] & (n_tok - 1)
            pltpu.make_async_copy(xn_hbm.at[pl.ds(tok, 1), :], gbuf[par].at[pl.ds(r, 1), :], gsem.at[par]).start()

    def wait_gather(par):
        pltpu.make_async_copy(xn_hbm.at[pl.ds(0, tm), :], gbuf[par], gsem.at[par]).wait()

    def compute(par):
        wg_ref, wu_ref, wd_ref = wts[par]
        xg = gbuf[par][...].astype(BF16)
        hg = jnp.dot(xg, wg_ref[0], preferred_element_type=F32)
        hu = jnp.dot(xg, wu_ref[0], preferred_element_type=F32)
        hh = ((hg * _sigmoid(hg)) * hu).astype(BF16)
        y_ref[par * tm:(par + 1) * tm, :] = jnp.dot(hh, wd_ref[0], preferred_element_type=F32)

    @pl.when(k == 0)
    def _():
        issue_gather(0, 0)

    wait_gather(0)
    issue_gather(2 * k + 1, 1)
    compute(0)
    wait_gather(1)
    issue_gather(2 * k + 2, 0)
    compute(1)

    @pl.when(k == nt // 2 - 1)
    def _():
        wait_gather(0)


def _moe_tiles(xn2, order, tile_expert, tile_base, w_gate, w_up, w_down, tm):
    n_tok = xn2.shape[0]
    assert n_tok & (n_tok - 1) == 0, "token count must be a power of two (pair id -> token by masking)"
    nt = tile_expert.shape[0] - 1
    assert nt % 2 == 0

    def wspec(shape, par):
        return pl.BlockSpec(shape, lambda k, te, base, order: (te[2 * k + par], 0, 0))

    w_specs = [wspec(s, par) for par in (0, 1)
               for s in ((1, D_MODEL, D_EXPERT), (1, D_MODEL, D_EXPERT), (1, D_EXPERT, D_MODEL))]
    return pl.pallas_call(
        functools.partial(_moe_tile_kernel, tm=tm, nt=nt, n_tok=n_tok),
        grid_spec=pltpu.PrefetchScalarGridSpec(
            num_scalar_prefetch=3,
            grid=(nt // 2,),
            in_specs=[pl.BlockSpec(memory_space=pl.ANY)] + w_specs,
            out_specs=pl.BlockSpec((2 * tm, D_MODEL), lambda k, te, base, order: (k, 0)),
            scratch_shapes=[
                pltpu.VMEM((tm, D_MODEL), F32),
                pltpu.VMEM((tm, D_MODEL), F32),
                pltpu.SemaphoreType.DMA((2,)),
            ],
        ),
        out_shape=jax.ShapeDtypeStruct((nt * tm, D_MODEL), F32),
        compiler_params=_params(("arbitrary",), disable_bounds_checks=True),
        name="moe_tiles",
    )(tile_expert, tile_base, order, xn2, w_gate, w_up, w_down, w_gate, w_up, w_down)


def _combine_kernel(dest_ref, x1_ref, wts_ref, ys_hbm, out_ref, ybuf, sem, *, tm, nblk, n_tok):
    i = pl.program_id(0)
    slot = i % 2

    def issue(blk, dst_slot):
        t0 = blk * tm
        for r in range(tm):
            for j in range(2):
                d = dest_ref[j * n_tok + t0 + r]
                pltpu.make_async_copy(ys_hbm.at[pl.ds(d, 1), :], ybuf.at[dst_slot, j, pl.ds(r, 1), :],
                                      sem.at[dst_slot]).start()

    def wait(w_slot):
        for j in range(2):
            pltpu.make_async_copy(ys_hbm.at[pl.ds(0, tm), :], ybuf.at[w_slot, j], sem.at[w_slot]).wait()

    @pl.when(i == 0)
    def _():
        issue(0, 0)

    issue(jnp.minimum(i + 1, nblk - 1), 1 - slot)
    wait(slot)
    w = wts_ref[...]
    out_ref[...] = x1_ref[...] + (w[:, 0:1] * ybuf[slot, 0] + w[:, 1:2] * ybuf[slot, 1])

    @pl.when(i == nblk - 1)
    def _():
        wait(1 - slot)


def _combine(x1, ys, dest, wts, tm):
    S = x1.shape[0]
    nblk = S // tm
    return pl.pallas_call(
        functools.partial(_combine_kernel, tm=tm, nblk=nblk, n_tok=S),
        grid_spec=pltpu.PrefetchScalarGridSpec(
            num_scalar_prefetch=1,
            grid=(nblk,),
            in_specs=[
                pl.BlockSpec((tm, D_MODEL), lambda i, dest: (i, 0)),
                pl.BlockSpec((tm, LANES), lambda i, dest: (i, 0)),
                pl.BlockSpec(memory_space=pl.ANY),
            ],
            out_specs=pl.BlockSpec((tm, D_MODEL), lambda i, dest: (i, 0)),
            scratch_shapes=[
                pltpu.VMEM((2, 2, tm, D_MODEL), F32),
                pltpu.SemaphoreType.DMA((2,)),
            ],
        ),
        out_shape=jax.ShapeDtypeStruct((S, D_MODEL), F32),
        compiler_params=_params(("arbitrary",), disable_bounds_checks=True),
        name="moe_combine",
    )(dest, x1, wts, ys)


def _moe(xn2, x1, ids, wts, w_gate, w_up, w_down):
    tm = MOE_TM
    order, tile_expert, tile_base, dest = _route_plan(ids[:, :2], tm)
    ys = _moe_tiles(xn2, order, tile_expert, tile_base, w_gate, w_up, w_down, tm)
    return _combine(x1, ys, dest, wts, _tile(x1.shape[0], MOE_TM))


def _tile(S, want):
    return min(want, S)


def _layer(x, p):
    S = x.shape[0]
    proj, gates = _inproj(x, p["g1"], p["w_main"], p["w_gate_in"], p["b_gate_in"], p["qk_gain"], _tile(S, 1024))
    y_na = _na_attention(proj, p["na_bias"])
    q_ml = _conv_silu(proj, p["conv_w"], p["conv_b"], _tile(S, 512), is_key=False)
    kt_ml = _conv_silu(proj, p["conv_w"], p["conv_b"], _tile(S, 512), is_key=True)
    G = 2
    gates_t = jnp.transpose(gates[:, :N_ML_GATES].reshape(S // ML_KCHUNK, ML_KCHUNK, N_ML_GATES), (0, 2, 1))
    hf, hb = _mlstm_scan(q_ml, kt_ml, proj, gates_t, G)
    merged = _merge(y_na, hf, hb, proj, p["b_branch"], p["hnorm_g"], p["w_na_out"], p["w_ml_out"], _tile(S, 512))
    x1, xn2, ids, wts = _outproj(merged, x, p["w_o"], p["g2"], p["w_router"], p["b_router"], _tile(S, 512))
    return _moe(xn2, x1, ids, wts, p["w_gate"], p["w_up"], p["w_down"])


def _prepare(l, norm1_g, w_in, b_ml_gates, b_branch, qn_g, kn_g, na_rpb, ml_conv_w, ml_conv_b, ml_hnorm_g,
             w_na_out, w_ml_out, w_o, norm2_g, w_router_group, b_router_group, w_router_expert,
             b_router_expert, w_gate, w_up, w_down):
    g_lo = 3 * NA_WIDTH + 4 * ML_WIDTH
    g_hi = g_lo + N_ML_GATES
    w = w_in[l]
    pad = LANES - N_ML_GATES
    n_r = N_GROUPS + N_EXPERTS
    w_r = jnp.pad(jnp.concatenate([w_router_group[l], w_router_expert[l]], axis=1), ((0, 0), (0, LANES - n_r)))
    w_r_hi = w_r.astype(BF16)
    w_r_lo = (w_r - w_r_hi.astype(F32)).astype(BF16)
    return {
        "g1": norm1_g[l].reshape(1, D_MODEL),
        "w_main": jnp.concatenate([w[:, :g_lo].astype(BF16), w[:, g_hi:].astype(BF16)], axis=1),
        "w_gate_in": jnp.pad(w[:, g_lo:g_hi], ((0, 0), (0, pad))).astype(BF16),
        "b_gate_in": jnp.pad(b_ml_gates[l], (0, pad)).reshape(1, LANES),
        "qk_gain": jnp.concatenate([qn_g[l].reshape(-1), kn_g[l].reshape(-1)]).reshape(1, 2 * NA_WIDTH),
        "na_bias": _na_bias_table(na_rpb[l]),
        "conv_w": ml_conv_w[l],
        "conv_b": ml_conv_b[l].reshape(1, 2 * ML_WIDTH),
        "b_branch": b_branch[l].reshape(1, 2 * D_MODEL),
        "hnorm_g": ml_hnorm_g[l].reshape(1, ML_WIDTH),
        "w_na_out": w_na_out[l].astype(BF16),
        "w_ml_out": w_ml_out[l].astype(BF16),
        "w_o": w_o[l].astype(BF16),
        "g2": norm2_g[l].reshape(1, D_MODEL),
        "w_router": jnp.concatenate([w_r_hi, w_r_lo], axis=1),
        "b_router": jnp.pad(jnp.concatenate([b_router_group[l], b_router_expert[l]]), (0, LANES - n_r)).reshape(1, LANES),
        "w_gate": w_gate[l].astype(BF16),
        "w_up": w_up[l].astype(BF16),
        "w_down": w_down[l].astype(BF16),
    }


def kernel(x_prompt, x_sample, norm1_g, w_in, b_ml_gates, b_branch, qn_g, kn_g, na_rpb, ml_conv_w, ml_conv_b,
           ml_hnorm_g, w_na_out, w_ml_out, w_o, norm2_g, w_router_group, b_router_group, w_router_expert,
           b_router_expert, w_gate, w_up, w_down):
    depth = w_in.shape[0]
    layers = [_prepare(l, norm1_g, w_in, b_ml_gates, b_branch, qn_g, kn_g, na_rpb, ml_conv_w, ml_conv_b,
                       ml_hnorm_g, w_na_out, w_ml_out, w_o, norm2_g, w_router_group, b_router_group,
                       w_router_expert, b_router_expert, w_gate, w_up, w_down) for l in range(depth)]

    def trunk(x):
        b, s, d = x.shape
        outs = []
        for bi in range(b):
            h = x[bi]
            for p in layers:
                h = _layer(h, p)
            outs.append(h)
        return jnp.stack(outs)

    return (trunk(x_prompt), trunk(x_sample))
```

```python
import functools

import numpy as np
import jax
import jax.numpy as jnp
from jax import lax
from jax.experimental import pallas as pl
from jax.experimental.pallas import tpu as pltpu

F32 = jnp.float32
BF16 = jnp.bfloat16

D_MODEL = 2048
GRID_W = 64
NA_HEADS = 8
NA_HEAD_DIM = 128
NA_WIDTH = NA_HEADS * NA_HEAD_DIM
NA_ROWS = 8
NA_COLS = 16
ML_HEADS = 4
ML_HEAD_DIM = 256
ML_WIDTH = ML_HEADS * ML_HEAD_DIM
ML_CHUNK = 64
N_ML_GATES = 4 * ML_HEADS
N_GROUPS = 4
EXPERTS_PER_GROUP = 4
N_EXPERTS = N_GROUPS * EXPERTS_PER_GROUP
D_EXPERT = 512
EPS = 1e-6

LANES = 128
COL_BLOCK = 1024
N_MAIN = 3 * NA_WIDTH + 4 * ML_WIDTH + 2 * D_MODEL
CB_NA_Q, CB_NA_K, CB_NA_V, CB_ML_Q, CB_ML_K, CB_ML_V, CB_ML_O, CB_BR = 0, 1, 2, 3, 4, 5, 6, 7
NEG_BIG = -1e30
VMEM_LIMIT = 56 * 1024 * 1024


def _sigmoid(x):
    return 0.5 * jnp.tanh(0.5 * x) + 0.5


def _params(sem, vmem=VMEM_LIMIT, **kw):
    return pltpu.CompilerParams(dimension_semantics=sem, vmem_limit_bytes=vmem, **kw)


def _inproj_kernel(x_ref, g1_ref, wa_ref, wb_ref, wg_ref, bg_ref, qkg_ref, proj_ref, gates_ref, xn_ref):
    j = pl.program_id(1)

    @pl.when(j == 0)
    def _():
        x = x_ref[...]
        ms = jnp.mean(x * x, axis=-1, keepdims=True)
        xn = ((x * lax.rsqrt(ms + EPS)) * g1_ref[...]).astype(BF16)
        xn_ref[...] = xn
        gates_ref[...] = jnp.dot(xn, wg_ref[...], preferred_element_type=F32) + bg_ref[...]

    @pl.when(j <= CB_NA_K)
    def _():
        acc = jnp.dot(xn_ref[...], wa_ref[...], preferred_element_type=F32)
        for h in range(NA_HEADS):
            sl = slice(h * NA_HEAD_DIM, (h + 1) * NA_HEAD_DIM)
            a = acc[:, sl]
            ms = jnp.mean(a * a, axis=-1, keepdims=True)
            proj_ref[:, sl] = ((a * lax.rsqrt(ms + EPS)) * qkg_ref[:, sl]).astype(BF16)

    @pl.when((j > CB_NA_K) & (j < CB_BR))
    def _():
        proj_ref[...] = jnp.dot(xn_ref[...], wa_ref[...], preferred_element_type=F32).astype(BF16)

    @pl.when(j >= CB_BR)
    def _():
        proj_ref[...] = jnp.dot(xn_ref[...], wb_ref[...], preferred_element_type=F32).astype(BF16)


def _inproj(x, g1, w_head, w_br, w_gate, b_gate, qk_gain, tm):
    S = x.shape[0]
    nj = N_MAIN // COL_BLOCK
    return pl.pallas_call(
        _inproj_kernel,
        grid=(S // tm, nj),
        in_specs=[
            pl.BlockSpec((tm, D_MODEL), lambda i, j: (i, 0)),
            pl.BlockSpec((1, D_MODEL), lambda i, j: (0, 0)),
            pl.BlockSpec((D_MODEL, COL_BLOCK), lambda i, j: (0, jnp.minimum(j, CB_BR - 1))),
            pl.BlockSpec((D_MODEL, COL_BLOCK), lambda i, j: (0, jnp.maximum(j - CB_BR, 0))),
            pl.BlockSpec((D_MODEL, LANES), lambda i, j: (0, 0)),
            pl.BlockSpec((1, LANES), lambda i, j: (0, 0)),
            pl.BlockSpec((1, COL_BLOCK), lambda i, j: (0, jnp.minimum(j, CB_NA_K))),
        ],
        out_specs=[
            pl.BlockSpec((tm, COL_BLOCK), lambda i, j: (i, j)),
            pl.BlockSpec((tm, LANES), lambda i, j: (i, 0)),
        ],
        out_shape=[
            jax.ShapeDtypeStruct((S, N_MAIN), BF16),
            jax.ShapeDtypeStruct((S, LANES), F32),
        ],
        scratch_shapes=[pltpu.VMEM((tm, D_MODEL), BF16)],
        compiler_params=_params(("parallel", "arbitrary")),
        name="inproj",
    )(x, g1, w_head, w_br, w_gate, b_gate, qk_gain)


NA_QROWS = 8
NA_QTOK = NA_QROWS * GRID_W
NA_KTOK = NA_ROWS * GRID_W


def _na_bias_table(rpb):
    c = np.arange(GRID_W)
    start_c = np.clip(c - NA_COLS // 2, 0, GRID_W - NA_COLS)
    kc = np.arange(GRID_W)
    mask = (kc[None, :] >= start_c[:, None]) & (kc[None, :] < start_c[:, None] + NA_COLS)
    dc = np.clip(kc[None, :] - c[:, None] + NA_COLS - 1, 0, 2 * NA_COLS - 2)
    e = jnp.where(mask[None, None], rpb.astype(F32)[:, :, dc], NEG_BIG)
    dr = np.arange(NA_ROWS)[None, :] - np.arange(NA_ROWS)[:, None] + NA_ROWS - 1
    b = e[:, dr]
    b = jnp.transpose(b, (1, 0, 3, 2, 4))
    return b.reshape(NA_ROWS * NA_HEADS, GRID_W, NA_KTOK)


def _na_kernel(q_ref, kp_ref, kc_ref, kn_ref, vp_ref, vc_ref, vn_ref, bias_ref, o_ref, kw_ref, vw_ref,
               s_ref, p_ref, l_ref, *, rows):
    b = pl.program_id(0)
    kw_ref[0:NA_QTOK] = kp_ref[...]
    kw_ref[NA_QTOK:2 * NA_QTOK] = kc_ref[...]
    kw_ref[2 * NA_QTOK:3 * NA_QTOK] = kn_ref[...]
    vw_ref[0:NA_QTOK] = vp_ref[...]
    vw_ref[NA_QTOK:2 * NA_QTOK] = vc_ref[...]
    vw_ref[2 * NA_QTOK:3 * NA_QTOK] = vn_ref[...]
    scale = NA_HEAD_DIM ** -0.5

    def row_body(rl, carry):
        r = b * NA_QROWS + rl
        rs = jnp.clip(r - NA_ROWS // 2, 0, rows - NA_ROWS)
        d = r - rs
        koff = pl.multiple_of((rs - (b - 1) * NA_QROWS) * GRID_W, GRID_W)
        qoff = pl.multiple_of(rl * GRID_W, GRID_W)
        heads = [slice(h * NA_HEAD_DIM, (h + 1) * NA_HEAD_DIM) for h in range(NA_HEADS)]
        for h, sl in enumerate(heads):
            q = q_ref[pl.ds(qoff, GRID_W), sl]
            k = kw_ref[pl.ds(koff, NA_KTOK), sl]
            s = lax.dot_general(q, k, (((1,), (1,)), ((), ())), preferred_element_type=F32)
            s_ref[h] = s * scale + bias_ref[d * NA_HEADS + h]
        for h in range(NA_HEADS):
            s = s_ref[h]
            p = jnp.exp(s - jnp.max(s, axis=-1, keepdims=True))
            l_ref[h] = jnp.sum(p, axis=-1, keepdims=True)
            p_ref[h] = p.astype(BF16)
        for h, sl in enumerate(heads):
            v = vw_ref[pl.ds(koff, NA_KTOK), sl]
            o = jnp.dot(p_ref[h], v, preferred_element_type=F32)
            o_ref[pl.ds(qoff, GRID_W), sl] = (o / l_ref[h]).astype(BF16)
        return carry

    lax.fori_loop(0, NA_QROWS, row_body, 0)


def _na_attention(proj, bias_tab):
    S = proj.shape[0]
    rows = S // GRID_W
    nb = rows // NA_QROWS
    blk = (NA_QTOK, COL_BLOCK)
    prev = lambda b: jnp.maximum(b - 1, 0)
    nxt = lambda b: jnp.minimum(b + 1, nb - 1)
    return pl.pallas_call(
        functools.partial(_na_kernel, rows=rows),
        grid=(nb,),
        in_specs=[
            pl.BlockSpec(blk, lambda b: (b, CB_NA_Q)),
            pl.BlockSpec(blk, lambda b: (prev(b), CB_NA_K)),
            pl.BlockSpec(blk, lambda b: (b, CB_NA_K)),
            pl.BlockSpec(blk, lambda b: (nxt(b), CB_NA_K)),
            pl.BlockSpec(blk, lambda b: (prev(b), CB_NA_V)),
            pl.BlockSpec(blk, lambda b: (b, CB_NA_V)),
            pl.BlockSpec(blk, lambda b: (nxt(b), CB_NA_V)),
            pl.BlockSpec((NA_ROWS * NA_HEADS, GRID_W, NA_KTOK), lambda b: (0, 0, 0)),
        ],
        out_specs=pl.BlockSpec(blk, lambda b: (b, 0)),
        out_shape=jax.ShapeDtypeStruct((S, NA_WIDTH), BF16),
        scratch_shapes=[
            pltpu.VMEM((3 * NA_QTOK, NA_WIDTH), BF16),
            pltpu.VMEM((3 * NA_QTOK, NA_WIDTH), BF16),
            pltpu.VMEM((NA_HEADS, GRID_W, NA_KTOK), F32),
            pltpu.VMEM((NA_HEADS, GRID_W, NA_KTOK), BF16),
            pltpu.VMEM((NA_HEADS, GRID_W, 1), F32),
        ],
        compiler_params=_params(("parallel",)),
        name="na_attention",
    )(proj, proj, proj, proj, proj, proj, proj, bias_tab)


HALO = 16


ML_KCHUNK = 256


def _conv_kernel(x_ref, xp_ref, xn_ref, w_ref, b_ref, o_ref, *, tb, nblk, is_key):
    i = pl.program_id(0)
    x = x_ref[...].astype(F32)
    prev_row = jnp.where(i > 0, xp_ref[HALO - 1:HALO, :].astype(F32), 0.0)
    next_row = jnp.where(i < nblk - 1, xn_ref[0:1, :].astype(F32), 0.0)
    ridx = lax.broadcasted_iota(jnp.int32, (tb, 1), 0)
    x_prev = jnp.where(ridx == 0, prev_row, pltpu.roll(x, 1, 0))
    x_next = jnp.where(ridx == tb - 1, next_row, pltpu.roll(x, tb - 1, 0))
    y = b_ref[...] + x_prev * w_ref[0:1, :] + x * w_ref[1:2, :] + x_next * w_ref[2:3, :]
    y = y * _sigmoid(y)
    if is_key:
        y = y * (ML_HEAD_DIM ** -0.5)
        for j in range(tb // ML_KCHUNK):
            o_ref[j] = y[j * ML_KCHUNK:(j + 1) * ML_KCHUNK, :].T.astype(BF16)
    else:
        o_ref[...] = y.astype(BF16)


def _conv_silu(proj, conv_w, conv_b, tb, is_key):
    S = proj.shape[0]
    nblk = S // tb
    hb = tb // HALO
    c = 1 if is_key else 0
    if is_key:
        out_spec = pl.BlockSpec((tb // ML_KCHUNK, ML_WIDTH, ML_KCHUNK), lambda i: (i, 0, 0))
        out_shape = jax.ShapeDtypeStruct((S // ML_KCHUNK, ML_WIDTH, ML_KCHUNK), BF16)
    else:
        out_spec = pl.BlockSpec((tb, ML_WIDTH), lambda i: (i, 0))
        out_shape = jax.ShapeDtypeStruct((S, ML_WIDTH), BF16)
    return pl.pallas_call(
        functools.partial(_conv_kernel, tb=tb, nblk=nblk, is_key=is_key),
        grid=(nblk,),
        in_specs=[
            pl.BlockSpec((tb, COL_BLOCK), lambda i: (i, CB_ML_Q + c)),
            pl.BlockSpec((HALO, COL_BLOCK), lambda i: (jnp.maximum(i * hb - 1, 0), CB_ML_Q + c)),
            pl.BlockSpec((HALO, COL_BLOCK), lambda i: (jnp.minimum((i + 1) * hb, S // HALO - 1), CB_ML_Q + c)),
            pl.BlockSpec((3, COL_BLOCK), lambda i: (0, c)),
            pl.BlockSpec((1, COL_BLOCK), lambda i: (0, c)),
        ],
        out_specs=out_spec,
        out_shape=out_shape,
        compiler_params=_params(("parallel",)),
        name="conv_silu_k" if is_key else "conv_silu_q",
    )(proj, proj, proj, conv_w, conv_b)


ML_AUG = ML_HEAD_DIM + LANES


def _log_sigmoid(x):
    return jnp.minimum(x, 0.0) - jnp.log1p(jnp.exp(-jnp.abs(x)))


def _split3(x):
    x1 = x.astype(BF16)
    r1 = x - x1.astype(F32)
    x2 = r1.astype(BF16)
    x3 = (r1 - x2.astype(F32)).astype(BF16)
    return x1, x2, x3


def _mlstm_kernel(qf_ref, ktf_ref, vf_ref, grf_ref, qb_ref, ktb_ref, vb_ref, grb_ref,
                  hf_ref, hb_ref, c_ref, m_ref, *, G):
    @pl.when(pl.program_id(0) == 0)
    def _():
        c_ref[...] = jnp.zeros_like(c_ref)
        m_ref[...] = jnp.zeros_like(m_ref)

    L = ML_KCHUNK
    ri = lax.broadcasted_iota(jnp.int32, (L, L), 0)
    ci = lax.broadcasted_iota(jnp.int32, (L, L), 1)
    tril = ci <= ri
    triu = ci >= ri
    trilb = tril.astype(BF16)
    triub = triu.astype(BF16)
    ones_blk = jnp.ones((L, LANES), BF16)
    pad_rows = jnp.zeros((LANES - N_ML_GATES, L), F32)

    def chain(dirn, cc):
        if dirn == 0:
            q_ref, kt_ref, v_ref, gr_ref, h_ref = qf_ref, ktf_ref, vf_ref, grf_ref, hf_ref
            tri_m, mask = triub, tril
        else:
            q_ref, kt_ref, v_ref, gr_ref, h_ref = qb_ref, ktb_ref, vb_ref, grb_ref, hb_ref
            tri_m, mask = trilb, triu
        ioff = dirn * 2 * ML_HEADS
        foff = ioff + ML_HEADS
        toff = pl.multiple_of(cc * L, L)
        grow = gr_ref[cc]
        br = jnp.dot(jnp.concatenate(_split3(_log_sigmoid(grow)), axis=0), tri_m, preferred_element_type=F32)
        brow_all = br[:N_ML_GATES] + br[N_ML_GATES:2 * N_ML_GATES] + br[2 * N_ML_GATES:]
        bcol_all = jnp.concatenate([brow_all, pad_rows], axis=0).T
        for h in range(ML_HEADS):
            ch = dirn * ML_HEADS + h
            m = m_ref[ch:ch + 1, 0:1]
            bcol = bcol_all[:, foff + h:foff + h + 1]
            brow = brow_all[foff + h:foff + h + 1, :]
            irow = grow[ioff + h:ioff + h + 1, :]
            log_d = jnp.where(mask, bcol - brow + irow, NEG_BIG)
            m_t = jnp.maximum(bcol + m, jnp.max(log_d, axis=-1, keepdims=True))
            dmat = jnp.exp(log_d - m_t)
            inter = jnp.exp(bcol + m - m_t)
            hs = slice(h * ML_HEAD_DIM, (h + 1) * ML_HEAD_DIM)
            q = q_ref[pl.ds(toff, L), hs]
            kt = kt_ref[cc, hs, :]
            v = v_ref[pl.ds(toff, L), hs]
            vaug = jnp.concatenate([v, ones_blk], axis=1)
            s = jnp.dot(q, kt, preferred_element_type=F32)
            sd = (s * dmat).astype(BF16)
            cst = c_ref[ch]
            num = (jnp.dot(sd, vaug, preferred_element_type=F32)
                   + inter * jnp.dot(q, cst.astype(BF16), preferred_element_type=F32))
            den = num[:, ML_HEAD_DIM:ML_HEAD_DIM + 1]
            h_ref[pl.ds(toff, L), hs] = (num[:, :ML_HEAD_DIM]
                                         / jnp.maximum(jnp.abs(den), jnp.exp(-m_t))).astype(BF16)
            g = brow[:, L - 1:L] if dirn == 0 else brow[:, 0:1]
            log_w = g - brow + irow
            m_new = jnp.maximum(g + m, jnp.max(log_w, axis=-1, keepdims=True))
            w = jnp.exp(log_w - m_new)
            decay = jnp.exp(g + m - m_new)
            kw = (kt.astype(F32) * w).astype(BF16)
            upd = jnp.dot(kw, vaug, preferred_element_type=F32)
            c_ref[ch] = decay * cst + upd
            m_ref[ch:ch + 1, :] = jnp.broadcast_to(m_new, (1, LANES))

    def chunk_body(c, carry):
        chain(0, c)
        chain(1, G - 1 - c)
        return carry

    lax.fori_loop(0, G, chunk_body, 0)


def _mlstm_scan(q, kt, proj, gates_t, G):
    S = q.shape[0]
    tb = G * ML_KCHUNK
    nb = S // tb
    fwd = lambda i: i
    bwd = lambda i: nb - 1 - i
    in_specs = []
    for mp in (fwd, bwd):
        in_specs += [
            pl.BlockSpec((tb, ML_WIDTH), lambda i, mp=mp: (mp(i), 0)),
            pl.BlockSpec((G, ML_WIDTH, ML_KCHUNK), lambda i, mp=mp: (mp(i), 0, 0)),
            pl.BlockSpec((tb, COL_BLOCK), lambda i, mp=mp: (mp(i), CB_ML_V)),
            pl.BlockSpec((G, N_ML_GATES, ML_KCHUNK), lambda i, mp=mp: (mp(i), 0, 0)),
        ]
    return pl.pallas_call(
        functools.partial(_mlstm_kernel, G=G),
        grid=(nb,),
        in_specs=in_specs,
        out_specs=[
            pl.BlockSpec((tb, ML_WIDTH), lambda i: (fwd(i), 0)),
            pl.BlockSpec((tb, ML_WIDTH), lambda i: (bwd(i), 0)),
        ],
        out_shape=[jax.ShapeDtypeStruct((S, ML_WIDTH), BF16), jax.ShapeDtypeStruct((S, ML_WIDTH), BF16)],
        scratch_shapes=[
            pltpu.VMEM((2 * ML_HEADS, ML_HEAD_DIM, ML_AUG), F32),
            pltpu.VMEM((2 * ML_HEADS, LANES), F32),
        ],
        compiler_params=_params(("arbitrary",)),
        name="mlstm_scan",
    )(q, kt, proj, gates_t, q, kt, proj, gates_t)


MERGE_NB = D_MODEL // COL_BLOCK


def _merge_kernel(yna_ref, hf_ref, hb_ref, o_ref, *rest):
    g_refs = rest[:2 * MERGE_NB]
    bias_ref, hg_ref, wna_ref, wml_ref, out_ref, yml_ref = rest[2 * MERGE_NB:]
    for h in range(ML_HEADS):
        sl = slice(h * ML_HEAD_DIM, (h + 1) * ML_HEAD_DIM)
        hh = hf_ref[:, sl].astype(F32) + hb_ref[:, sl].astype(F32)
        ms = jnp.mean(hh * hh, axis=-1, keepdims=True)
        y = (hh * lax.rsqrt(ms + EPS)) * hg_ref[:, sl]
        yml_ref[:, sl] = (y * _sigmoid(o_ref[:, sl].astype(F32))).astype(BF16)
    for n in range(MERGE_NB):
        cols = slice(n * COL_BLOCK, (n + 1) * COL_BLOCK)
        ga = _sigmoid(g_refs[n][...].astype(F32) + bias_ref[:, cols])
        gb = _sigmoid(g_refs[MERGE_NB + n][...].astype(F32) + bias_ref[:, D_MODEL + n * COL_BLOCK:
                                                                       D_MODEL + (n + 1) * COL_BLOCK])
        a = jnp.dot(yna_ref[...], wna_ref[:, cols], preferred_element_type=F32)
        bm = jnp.dot(yml_ref[...], wml_ref[:, cols], preferred_element_type=F32)
        out_ref[:, cols] = (ga * a + gb * bm).astype(BF16)


def _merge(y_na, hf, hb, proj, b_branch, hnorm_g, w_na_out, w_ml_out, tm):
    S = y_na.shape[0]
    const = lambda shape: pl.BlockSpec(shape, lambda i: (0, 0), pipeline_mode=pl.Buffered(1))
    gate_specs = [pl.BlockSpec((tm, COL_BLOCK), lambda i, c=c: (i, CB_BR + c)) for c in range(2 * MERGE_NB)]
    return pl.pallas_call(
        _merge_kernel,
        grid=(S // tm,),
        in_specs=[
            pl.BlockSpec((tm, NA_WIDTH), lambda i: (i, 0)),
            pl.BlockSpec((tm, ML_WIDTH), lambda i: (i, 0)),
            pl.BlockSpec((tm, ML_WIDTH), lambda i: (i, 0)),
            pl.BlockSpec((tm, COL_BLOCK), lambda i: (i, CB_ML_O)),
        ] + gate_specs + [
            const((1, 2 * D_MODEL)),
            const((1, ML_WIDTH)),
            const((NA_WIDTH, D_MODEL)),
            const((ML_WIDTH, D_MODEL)),
        ],
        out_specs=pl.BlockSpec((tm, D_MODEL), lambda i: (i, 0)),
        out_shape=jax.ShapeDtypeStruct((S, D_MODEL), BF16),
        scratch_shapes=[pltpu.VMEM((tm, ML_WIDTH), BF16)],
        compiler_params=_params(("parallel",)),
        name="merge",
    )(y_na, hf, hb, proj, *([proj] * (2 * MERGE_NB)), b_branch, hnorm_g, w_na_out, w_ml_out)


def _outproj_kernel(mg_ref, x_ref, wo_ref, g2_ref, wr_ref, br_ref, x1_ref, xn_ref, ids_ref, wts_ref):
    x1 = x_ref[...] + jnp.dot(mg_ref[...], wo_ref[...], preferred_element_type=F32)
    x1_ref[...] = x1
    ms = jnp.mean(x1 * x1, axis=-1, keepdims=True)
    xn = (x1 * lax.rsqrt(ms + EPS)) * g2_ref[...]
    xn_ref[...] = xn
    xh = xn.astype(BF16)
    xl = (xn - xh.astype(F32)).astype(BF16)
    hl = jnp.dot(xh, wr_ref[...], preferred_element_type=F32)
    logits = (hl[:, :LANES] + hl[:, LANES:]
              + jnp.dot(xl, wr_ref[:, :LANES], preferred_element_type=F32) + br_ref[...])
    tm = logits.shape[0]
    lane = lax.broadcasted_iota(jnp.int32, (tm, LANES), 1)
    big = jnp.int32(1 << 20)
    gmask = lane < N_GROUPS
    lg = jnp.where(gmask, logits, NEG_BIG)
    gmax = jnp.max(lg, axis=-1, keepdims=True)
    p_group = 1.0 / jnp.sum(jnp.exp(lg - gmax), axis=-1, keepdims=True)
    g_idx = jnp.min(jnp.where(lg == gmax, lane, big), axis=-1, keepdims=True)
    e_lo = N_GROUPS + g_idx * EXPERTS_PER_GROUP
    emask = (lane >= e_lo) & (lane < e_lo + EXPERTS_PER_GROUP)
    le = jnp.where(emask, logits, NEG_BIG)
    m1 = jnp.max(le, axis=-1, keepdims=True)
    i1 = jnp.min(jnp.where(le == m1, lane, big), axis=-1, keepdims=True)
    le2 = jnp.where(lane == i1, NEG_BIG, le)
    m2 = jnp.max(le2, axis=-1, keepdims=True)
    i2 = jnp.min(jnp.where(le2 == m2, lane, big), axis=-1, keepdims=True)
    e2 = jnp.exp(m2 - m1)
    w1 = p_group / (1.0 + e2)
    w2 = p_group * e2 / (1.0 + e2)
    ids_ref[...] = jnp.where(lane == 0, i1 - N_GROUPS, jnp.where(lane == 1, i2 - N_GROUPS, 0))
    wts_ref[...] = jnp.where(lane == 0, w1, jnp.where(lane == 1, w2, 0.0))


def _outproj(merged, x, w_o, g2, w_router, b_router, tm):
    S = x.shape[0]
    return pl.pallas_call(
        _outproj_kernel,
        grid=(S // tm,),
        in_specs=[
            pl.BlockSpec((tm, D_MODEL), lambda i: (i, 0)),
            pl.BlockSpec((tm, D_MODEL), lambda i: (i, 0)),
            pl.BlockSpec((D_MODEL, D_MODEL), lambda i: (0, 0), pipeline_mode=pl.Buffered(1)),
            pl.BlockSpec((1, D_MODEL), lambda i: (0, 0)),
            pl.BlockSpec((D_MODEL, 2 * LANES), lambda i: (0, 0)),
            pl.BlockSpec((1, LANES), lambda i: (0, 0)),
        ],
        out_specs=[
            pl.BlockSpec((tm, D_MODEL), lambda i: (i, 0)),
            pl.BlockSpec((tm, D_MODEL), lambda i: (i, 0)),
            pl.BlockSpec((tm, LANES), lambda i: (i, 0)),
            pl.BlockSpec((tm, LANES), lambda i: (i, 0)),
        ],
        out_shape=[
            jax.ShapeDtypeStruct((S, D_MODEL), F32),
            jax.ShapeDtypeStruct((S, D_MODEL), F32),
            jax.ShapeDtypeStruct((S, LANES), jnp.int32),
            jax.ShapeDtypeStruct((S, LANES), F32),
        ],
        compiler_params=_params(("parallel",)),
        name="outproj_router",
    )(merged, x, w_o, g2, w_router, b_router)


MOE_TM = 256


def _route_plan(ids, tm):
    n_tok = ids.shape[0]
    n_pairs = 2 * n_tok
    nt = n_pairs // tm + N_EXPERTS
    assert n_pairs & (n_pairs - 1) == 0 and nt * tm <= (1 << 16), "pair ids / rows are packed into 16-bit fields"
    e = ids.T.reshape(-1)
    pid = jnp.arange(n_pairs, dtype=jnp.int32)
    order = lax.sort(e * n_pairs + pid) & (n_pairs - 1)
    experts = jnp.arange(N_EXPERTS, dtype=jnp.int32)
    cnt = jnp.sum((e[None, :] == experts[:, None]).astype(jnp.int32), axis=1)
    tiles_e = (cnt + tm - 1) // tm
    tend = jnp.cumsum(tiles_e)
    tstart = tend - tiles_e
    cend = jnp.cumsum(cnt)
    cstart = cend - cnt
    tile_idx = jnp.arange(nt, dtype=jnp.int32)
    te = jnp.minimum(jnp.sum((tile_idx[:, None] >= tend[None, :]).astype(jnp.int32), axis=1), N_EXPERTS - 1)
    local = (tile_idx - tstart[te]) * tm
    base = jnp.where(cnt[te] - local > 0, cstart[te] + local, 0)
    off = tstart * tm - cstart
    steps = jnp.where(pid[None, :] >= cend[:-1, None], (off[1:] - off[:-1])[:, None], 0)
    row_of_pos = pid + off[0] + jnp.sum(steps, axis=0)
    dest = lax.sort(order.astype(jnp.uint32) * jnp.uint32(1 << 16) + row_of_pos.astype(jnp.uint32))
    dest = (dest & jnp.uint32(0xFFFF)).astype(jnp.int32)
    return jnp.concatenate([order, pid[:tm]]), te.astype(jnp.int32), base.astype(jnp.int32), tend[-1:], dest


def _moe_tile_kernel(te_ref, base_ref, nact_ref, order_ref, xn_hbm, wg0_ref, wu0_ref, wd0_ref, wg1_ref, wu1_ref,
                     wd1_ref, y_ref, gbuf0, gbuf1, gsem, *, tm, n_tok):
    k = pl.program_id(0)
    n_active = nact_ref[0]
    gbuf = (gbuf0, gbuf1)
    wts = ((wg0_ref, wu0_ref, wd0_ref), (wg1_ref, wu1_ref, wd1_ref))

    def issue_gather(tile, par):
        first = base_ref[tile]
        for r in range(tm):
            tok = order_ref[first + r] & (n_tok - 1)
            pltpu.make_async_copy(xn_hbm.at[pl.ds(tok, 1), :], gbuf[par].at[pl.ds(r, 1), :], gsem.at[par]).start()

    def wait_gather(par):
        pltpu.make_async_copy(xn_hbm.at[pl.ds(0, tm), :], gbuf[par], gsem.at[par]).wait()

    def half_step(tile, par):
        rows = slice(par * tm, (par + 1) * tm)

        @pl.when(tile < n_active)
        def _():
            wait_gather(par)

        @pl.when(tile + 1 < n_active)
        def _():
            issue_gather(tile + 1, 1 - par)

        @pl.when(tile < n_active)
        def _():
            wg_ref, wu_ref, wd_ref = wts[par]
            xg = gbuf[par][...].astype(BF16)
            hg = jnp.dot(xg, wg_ref[0], preferred_element_type=F32)
            hu = jnp.dot(xg, wu_ref[0], preferred_element_type=F32)
            hh = ((hg * _sigmoid(hg)) * hu).astype(BF16)
            y_ref[rows, :] = jnp.dot(hh, wd_ref[0], preferred_element_type=F32)

        @pl.when(tile >= n_active)
        def _():
            y_ref[rows, :] = jnp.zeros((tm, D_MODEL), F32)

    @pl.when((k == 0) & (n_active > 0))
    def _():
        issue_gather(0, 0)

    half_step(2 * k, 0)
    half_step(2 * k + 1, 1)


def _moe_tiles(xn2, order, tile_expert, tile_base, n_active, w_gate, w_up, w_down, tm):
    n_tok = xn2.shape[0]
    assert n_tok & (n_tok - 1) == 0, "token count must be a power of two (pair id -> token by masking)"
    nt = tile_expert.shape[0]
    assert nt % 2 == 0

    def wspec(shape, par):
        return pl.BlockSpec(shape, lambda k, te, base, nact, order: (te[2 * k + par], 0, 0))

    w_specs = [wspec(s, par) for par in (0, 1)
               for s in ((1, D_MODEL, D_EXPERT), (1, D_MODEL, D_EXPERT), (1, D_EXPERT, D_MODEL))]
    return pl.pallas_call(
        functools.partial(_moe_tile_kernel, tm=tm, n_tok=n_tok),
        grid_spec=pltpu.PrefetchScalarGridSpec(
            num_scalar_prefetch=4,
            grid=(nt // 2,),
            in_specs=[pl.BlockSpec(memory_space=pl.ANY)] + w_specs,
            out_specs=pl.BlockSpec((2 * tm, D_MODEL), lambda k, te, base, nact, order: (k, 0)),
            scratch_shapes=[
                pltpu.VMEM((tm, D_MODEL), F32),
                pltpu.VMEM((tm, D_MODEL), F32),
                pltpu.SemaphoreType.DMA((2,)),
            ],
        ),
        out_shape=jax.ShapeDtypeStruct((nt * tm, D_MODEL), F32),
        compiler_params=_params(("arbitrary",), disable_bounds_checks=True),
        name="moe_tiles",
    )(tile_expert, tile_base, n_active, order, xn2, w_gate, w_up, w_down, w_gate, w_up, w_down)


def _combine_kernel(dest_ref, x1_ref, wts_ref, ys_hbm, out_ref, ybuf, sem, *, tm, nblk, n_tok):
    i = pl.program_id(0)
    slot = i % 2

    def issue(blk, dst_slot):
        t0 = blk * tm
        for r in range(tm):
            for j in range(2):
                d = dest_ref[j * n_tok + t0 + r]
                pltpu.make_async_copy(ys_hbm.at[pl.ds(d, 1), :], ybuf.at[dst_slot, j, pl.ds(r, 1), :],
                                      sem.at[dst_slot]).start()

    def wait(w_slot):
        for j in range(2):
            pltpu.make_async_copy(ys_hbm.at[pl.ds(0, tm), :], ybuf.at[w_slot, j], sem.at[w_slot]).wait()

    @pl.when(i == 0)
    def _():
        issue(0, 0)

    issue(jnp.minimum(i + 1, nblk - 1), 1 - slot)
    wait(slot)
    w = wts_ref[...]
    out_ref[...] = x1_ref[...] + (w[:, 0:1] * ybuf[slot, 0] + w[:, 1:2] * ybuf[slot, 1])

    @pl.when(i == nblk - 1)
    def _():
        wait(1 - slot)


def _combine(x1, ys, dest, wts, tm):
    S = x1.shape[0]
    nblk = S // tm
    return pl.pallas_call(
        functools.partial(_combine_kernel, tm=tm, nblk=nblk, n_tok=S),
        grid_spec=pltpu.PrefetchScalarGridSpec(
            num_scalar_prefetch=1,
            grid=(nblk,),
            in_specs=[
                pl.BlockSpec((tm, D_MODEL), lambda i, dest: (i, 0)),
                pl.BlockSpec((tm, LANES), lambda i, dest: (i, 0)),
                pl.BlockSpec(memory_space=pl.ANY),
            ],
            out_specs=pl.BlockSpec((tm, D_MODEL), lambda i, dest: (i, 0)),
            scratch_shapes=[
                pltpu.VMEM((2, 2, tm, D_MODEL), F32),
                pltpu.SemaphoreType.DMA((2,)),
            ],
        ),
        out_shape=jax.ShapeDtypeStruct((S, D_MODEL), F32),
        compiler_params=_params(("arbitrary",), disable_bounds_checks=True),
        name="moe_combine",
    )(dest, x1, wts, ys)


def _moe(xn2, x1, ids, wts, w_gate, w_up, w_down):
    tm = MOE_TM
    order, tile_expert, tile_base, n_active, dest = _route_plan(ids[:, :2], tm)
    ys = _moe_tiles(xn2, order, tile_expert, tile_base, n_active, w_gate, w_up, w_down, tm)
    return _combine(x1, ys, dest, wts, _tile(x1.shape[0], MOE_TM))


def _tile(S, want):
    return min(want, S)


def _layer(x, p):
    S = x.shape[0]
    proj, gates = _inproj(x, p["g1"], p["w_head"], p["w_br"], p["w_gate_in"], p["b_gate_in"], p["qk_gain"],
                          _tile(S, 1024))
    y_na = _na_attention(proj, p["na_bias"])
    q_ml = _conv_silu(proj, p["conv_w"], p["conv_b"], _tile(S, 512), is_key=False)
    kt_ml = _conv_silu(proj, p["conv_w"], p["conv_b"], _tile(S, 512), is_key=True)
    G = 2
    gates_t = jnp.transpose(gates[:, :N_ML_GATES].reshape(S // ML_KCHUNK, ML_KCHUNK, N_ML_GATES), (0, 2, 1))
    hf, hb = _mlstm_scan(q_ml, kt_ml, proj, gates_t, G)
    merged = _merge(y_na, hf, hb, proj, p["b_branch"], p["hnorm_g"], p["w_na_out"], p["w_ml_out"], _tile(S, 512))
    x1, xn2, ids, wts = _outproj(merged, x, p["w_o"], p["g2"], p["w_router"], p["b_router"], _tile(S, 512))
    return _moe(xn2, x1, ids, wts, p["w_gate"], p["w_up"], p["w_down"])


def _prepare(l, norm1_g, w_in, b_ml_gates, b_branch, qn_g, kn_g, na_rpb, ml_conv_w, ml_conv_b, ml_hnorm_g,
             w_na_out, w_ml_out, w_o, norm2_g, w_router_group, b_router_group, w_router_expert,
             b_router_expert, w_gate, w_up, w_down):
    g_lo = 3 * NA_WIDTH + 4 * ML_WIDTH
    g_hi = g_lo + N_ML_GATES
    w = w_in[l]
    pad = LANES - N_ML_GATES
    n_r = N_GROUPS + N_EXPERTS
    w_r = jnp.pad(jnp.concatenate([w_router_group[l], w_router_expert[l]], axis=1), ((0, 0), (0, LANES - n_r)))
    w_r_hi = w_r.astype(BF16)
    w_r_lo = (w_r - w_r_hi.astype(F32)).astype(BF16)
    return {
        "g1": norm1_g[l].reshape(1, D_MODEL),
        "w_head": w[:, :g_lo].astype(BF16),
        "w_br": w[:, g_hi:].astype(BF16),
        "w_gate_in": jnp.pad(w[:, g_lo:g_hi], ((0, 0), (0, pad))).astype(BF16),
        "b_gate_in": jnp.pad(b_ml_gates[l], (0, pad)).reshape(1, LANES),
        "qk_gain": jnp.concatenate([qn_g[l].reshape(-1), kn_g[l].reshape(-1)]).reshape(1, 2 * NA_WIDTH),
        "na_bias": _na_bias_table(na_rpb[l]),
        "conv_w": ml_conv_w[l],
        "conv_b": ml_conv_b[l].reshape(1, 2 * ML_WIDTH),
        "b_branch": b_branch[l].reshape(1, 2 * D_MODEL),
        "hnorm_g": ml_hnorm_g[l].reshape(1, ML_WIDTH),
        "w_na_out": w_na_out[l].astype(BF16),
        "w_ml_out": w_ml_out[l].astype(BF16),
        "w_o": w_o[l].astype(BF16),
        "g2": norm2_g[l].reshape(1, D_MODEL),
        "w_router": jnp.concatenate([w_r_hi, w_r_lo], axis=1),
        "b_router": jnp.pad(jnp.concatenate([b_router_group[l], b_router_expert[l]]), (0, LANES - n_r)).reshape(1, LANES),
        "w_gate": w_gate[l].astype(BF16),
        "w_up": w_up[l].astype(BF16),
        "w_down": w_down[l].astype(BF16),
    }


def kernel(x_prompt, x_sample, norm1_g, w_in, b_ml_gates, b_branch, qn_g, kn_g, na_rpb, ml_conv_w, ml_conv_b,
           ml_hnorm_g, w_na_out, w_ml_out, w_o, norm2_g, w_router_group, b_router_group, w_router_expert,
           b_router_expert, w_gate, w_up, w_down):
    depth = w_in.shape[0]
    layers = [_prepare(l, norm1_g, w_in, b_ml_gates, b_branch, qn_g, kn_g, na_rpb, ml_conv_w, ml_conv_b,
                       ml_hnorm_g, w_na_out, w_ml_out, w_o, norm2_g, w_router_group, b_router_group,
                       w_router_expert, b_router_expert, w_gate, w_up, w_down) for l in range(depth)]

    def trunk(x):
        b, s, d = x.shape
        outs = []
        for bi in range(b):
            h = x.reshape(s, d) if b == 1 else x[bi]
            for p in layers:
                h = _layer(h, p)
            outs.append(h)
        return outs[0].reshape(b, s, d) if b == 1 else jnp.stack(outs)

    return (trunk(x_prompt), trunk(x_sample))
```

```python
import functools

import numpy as np
import jax
import jax.numpy as jnp
from jax import lax
from jax.experimental import pallas as pl
from jax.experimental.pallas import tpu as pltpu

F32 = jnp.float32
BF16 = jnp.bfloat16

D_MODEL = 2048
GRID_W = 64
NA_HEADS = 8
NA_HEAD_DIM = 128
NA_WIDTH = NA_HEADS * NA_HEAD_DIM
NA_ROWS = 8
NA_COLS = 16
ML_HEADS = 4
ML_HEAD_DIM = 256
ML_WIDTH = ML_HEADS * ML_HEAD_DIM
ML_CHUNK = 64
N_ML_GATES = 4 * ML_HEADS
N_GROUPS = 4
EXPERTS_PER_GROUP = 4
N_EXPERTS = N_GROUPS * EXPERTS_PER_GROUP
D_EXPERT = 512
EPS = 1e-6

LANES = 128
COL_BLOCK = 1024
N_MAIN = 3 * NA_WIDTH + 4 * ML_WIDTH + 2 * D_MODEL
CB_NA_Q, CB_NA_K, CB_NA_V, CB_ML_Q, CB_ML_K, CB_ML_V, CB_ML_O, CB_BR = 0, 1, 2, 3, 4, 5, 6, 7
NEG_BIG = -1e30
VMEM_LIMIT = 56 * 1024 * 1024


def _sigmoid(x):
    return 0.5 * jnp.tanh(0.5 * x) + 0.5


def _params(sem, vmem=VMEM_LIMIT, **kw):
    return pltpu.CompilerParams(dimension_semantics=sem, vmem_limit_bytes=vmem, **kw)


def _wbr_kernel(a_ref, b_ref, o_ref):
    o_ref[...] = jnp.concatenate([a_ref[:, N_ML_GATES:], b_ref[:, :N_ML_GATES]], axis=1).astype(BF16)


def _branch_gate_weights(w, tr=512):
    g_lo = 3 * NA_WIDTH + 4 * ML_WIDTH
    assert g_lo % COL_BLOCK == 0
    cb = g_lo // COL_BLOCK
    return pl.pallas_call(
        _wbr_kernel,
        grid=(D_MODEL // tr, 2 * D_MODEL // COL_BLOCK),
        in_specs=[
            pl.BlockSpec((tr, COL_BLOCK), lambda i, j: (i, cb + j)),
            pl.BlockSpec((tr, COL_BLOCK), lambda i, j: (i, cb + j + 1)),
        ],
        out_specs=pl.BlockSpec((tr, COL_BLOCK), lambda i, j: (i, j)),
        out_shape=jax.ShapeDtypeStruct((D_MODEL, 2 * D_MODEL), BF16),
        compiler_params=_params(("parallel", "parallel")),
        name="branch_gate_weights",
    )(w, w)


def _inproj_kernel(x_ref, g1_ref, wa_ref, wb_ref, wg_ref, bg_ref, qkg_ref, proj_ref, gates_ref, xn_ref):
    j = pl.program_id(1)

    @pl.when(j == 0)
    def _():
        x = x_ref[...]
        ms = jnp.mean(x * x, axis=-1, keepdims=True)
        xn = ((x * lax.rsqrt(ms + EPS)) * g1_ref[...]).astype(BF16)
        xn_ref[...] = xn
        gates_ref[...] = jnp.dot(xn, wg_ref[...], preferred_element_type=F32) + bg_ref[...]

    @pl.when(j <= CB_NA_K)
    def _():
        acc = jnp.dot(xn_ref[...], wa_ref[...], preferred_element_type=F32)
        for h in range(NA_HEADS):
            sl = slice(h * NA_HEAD_DIM, (h + 1) * NA_HEAD_DIM)
            a = acc[:, sl]
            ms = jnp.mean(a * a, axis=-1, keepdims=True)
            proj_ref[:, sl] = ((a * lax.rsqrt(ms + EPS)) * qkg_ref[:, sl]).astype(BF16)

    @pl.when((j > CB_NA_K) & (j < CB_BR))
    def _():
        proj_ref[...] = jnp.dot(xn_ref[...], wa_ref[...], preferred_element_type=F32).astype(BF16)

    @pl.when(j >= CB_BR)
    def _():
        proj_ref[...] = jnp.dot(xn_ref[...], wb_ref[...], preferred_element_type=F32).astype(BF16)


def _inproj(x, g1, w_head, w_br, w_gate, b_gate, qk_gain, tm):
    S = x.shape[0]
    nj = N_MAIN // COL_BLOCK
    return pl.pallas_call(
        _inproj_kernel,
        grid=(S // tm, nj),
        in_specs=[
            pl.BlockSpec((tm, D_MODEL), lambda i, j: (i, 0)),
            pl.BlockSpec((1, D_MODEL), lambda i, j: (0, 0)),
            pl.BlockSpec((D_MODEL, COL_BLOCK), lambda i, j: (0, jnp.minimum(j, CB_BR - 1))),
            pl.BlockSpec((D_MODEL, COL_BLOCK), lambda i, j: (0, jnp.maximum(j - CB_BR, 0))),
            pl.BlockSpec((D_MODEL, LANES), lambda i, j: (0, 0)),
            pl.BlockSpec((1, LANES), lambda i, j: (0, 0)),
            pl.BlockSpec((1, COL_BLOCK), lambda i, j: (0, jnp.minimum(j, CB_NA_K))),
        ],
        out_specs=[
            pl.BlockSpec((tm, COL_BLOCK), lambda i, j: (i, j)),
            pl.BlockSpec((tm, LANES), lambda i, j: (i, 0)),
        ],
        out_shape=[
            jax.ShapeDtypeStruct((S, N_MAIN), BF16),
            jax.ShapeDtypeStruct((S, LANES), F32),
        ],
        scratch_shapes=[pltpu.VMEM((tm, D_MODEL), BF16)],
        compiler_params=_params(("parallel", "arbitrary")),
        name="inproj",
    )(x, g1, w_head, w_br, w_gate, b_gate, qk_gain)


NA_QROWS = 8
NA_QTOK = NA_QROWS * GRID_W
NA_KTOK = NA_ROWS * GRID_W


def _na_bias_table(rpb):
    c = np.arange(GRID_W)
    start_c = np.clip(c - NA_COLS // 2, 0, GRID_W - NA_COLS)
    kc = np.arange(GRID_W)
    mask = (kc[None, :] >= start_c[:, None]) & (kc[None, :] < start_c[:, None] + NA_COLS)
    dc = kc[None, :] - c[:, None] + NA_COLS - 1
    onehot = ((dc[None] == np.arange(2 * NA_COLS - 1)[:, None, None]) & mask[None]).astype(np.float32)
    e = jnp.einsum("hrd,dck->hrck", rpb.astype(F32), onehot, precision=lax.Precision.HIGHEST)
    e = jnp.where(mask[None, None], e, NEG_BIG)
    b = jnp.stack([e[:, NA_ROWS - 1 - d:2 * NA_ROWS - 1 - d] for d in range(NA_ROWS)], axis=0)
    b = jnp.transpose(b, (0, 1, 3, 2, 4))
    return b.reshape(NA_ROWS * NA_HEADS, GRID_W, NA_KTOK)


def _na_kernel(q_ref, kp_ref, kc_ref, kn_ref, vp_ref, vc_ref, vn_ref, bias_ref, o_ref, kw_ref, vw_ref,
               s_ref, p_ref, l_ref, *, rows):
    b = pl.program_id(0)
    kw_ref[0:NA_QTOK] = kp_ref[...]
    kw_ref[NA_QTOK:2 * NA_QTOK] = kc_ref[...]
    kw_ref[2 * NA_QTOK:3 * NA_QTOK] = kn_ref[...]
    vw_ref[0:NA_QTOK] = vp_ref[...]
    vw_ref[NA_QTOK:2 * NA_QTOK] = vc_ref[...]
    vw_ref[2 * NA_QTOK:3 * NA_QTOK] = vn_ref[...]
    scale = NA_HEAD_DIM ** -0.5

    def row_body(rl, carry):
        r = b * NA_QROWS + rl
        rs = jnp.clip(r - NA_ROWS // 2, 0, rows - NA_ROWS)
        d = r - rs
        koff = pl.multiple_of((rs - (b - 1) * NA_QROWS) * GRID_W, GRID_W)
        qoff = pl.multiple_of(rl * GRID_W, GRID_W)
        heads = [slice(h * NA_HEAD_DIM, (h + 1) * NA_HEAD_DIM) for h in range(NA_HEADS)]
        for h, sl in enumerate(heads):
            q = q_ref[pl.ds(qoff, GRID_W), sl]
            k = kw_ref[pl.ds(koff, NA_KTOK), sl]
            s = lax.dot_general(q, k, (((1,), (1,)), ((), ())), preferred_element_type=F32)
            s_ref[h] = s * scale + bias_ref[d * NA_HEADS + h]
        for h in range(NA_HEADS):
            s = s_ref[h]
            p = jnp.exp(s - jnp.max(s, axis=-1, keepdims=True))
            l_ref[h] = jnp.sum(p, axis=-1, keepdims=True)
            p_ref[h] = p.astype(BF16)
        for h, sl in enumerate(heads):
            v = vw_ref[pl.ds(koff, NA_KTOK), sl]
            o = jnp.dot(p_ref[h], v, preferred_element_type=F32)
            o_ref[pl.ds(qoff, GRID_W), sl] = (o / l_ref[h]).astype(BF16)
        return carry

    lax.fori_loop(0, NA_QROWS, row_body, 0)


def _na_attention(proj, bias_tab):
    S = proj.shape[0]
    rows = S // GRID_W
    nb = rows // NA_QROWS
    blk = (NA_QTOK, COL_BLOCK)
    prev = lambda b: jnp.maximum(b - 1, 0)
    nxt = lambda b: jnp.minimum(b + 1, nb - 1)
    return pl.pallas_call(
        functools.partial(_na_kernel, rows=rows),
        grid=(nb,),
        in_specs=[
            pl.BlockSpec(blk, lambda b: (b, CB_NA_Q)),
            pl.BlockSpec(blk, lambda b: (prev(b), CB_NA_K)),
            pl.BlockSpec(blk, lambda b: (b, CB_NA_K)),
            pl.BlockSpec(blk, lambda b: (nxt(b), CB_NA_K)),
            pl.BlockSpec(blk, lambda b: (prev(b), CB_NA_V)),
            pl.BlockSpec(blk, lambda b: (b, CB_NA_V)),
            pl.BlockSpec(blk, lambda b: (nxt(b), CB_NA_V)),
            pl.BlockSpec((NA_ROWS * NA_HEADS, GRID_W, NA_KTOK), lambda b: (0, 0, 0)),
        ],
        out_specs=pl.BlockSpec(blk, lambda b: (b, 0)),
        out_shape=jax.ShapeDtypeStruct((S, NA_WIDTH), BF16),
        scratch_shapes=[
            pltpu.VMEM((3 * NA_QTOK, NA_WIDTH), BF16),
            pltpu.VMEM((3 * NA_QTOK, NA_WIDTH), BF16),
            pltpu.VMEM((NA_HEADS, GRID_W, NA_KTOK), F32),
            pltpu.VMEM((NA_HEADS, GRID_W, NA_KTOK), BF16),
            pltpu.VMEM((NA_HEADS, GRID_W, 1), F32),
        ],
        compiler_params=_params(("parallel",)),
        name="na_attention",
    )(proj, proj, proj, proj, proj, proj, proj, bias_tab)


HALO = 16


ML_KCHUNK = 256


def _conv_kernel(x_ref, xp_ref, xn_ref, w_ref, b_ref, o_ref, *, tb, nblk, is_key):
    i = pl.program_id(0)
    x = x_ref[...].astype(F32)
    prev_row = jnp.where(i > 0, xp_ref[HALO - 1:HALO, :].astype(F32), 0.0)
    next_row = jnp.where(i < nblk - 1, xn_ref[0:1, :].astype(F32), 0.0)
    ridx = lax.broadcasted_iota(jnp.int32, (tb, 1), 0)
    x_prev = jnp.where(ridx == 0, prev_row, pltpu.roll(x, 1, 0))
    x_next = jnp.where(ridx == tb - 1, next_row, pltpu.roll(x, tb - 1, 0))
    y = b_ref[...] + x_prev * w_ref[0:1, :] + x * w_ref[1:2, :] + x_next * w_ref[2:3, :]
    y = y * _sigmoid(y)
    if is_key:
        y = y * (ML_HEAD_DIM ** -0.5)
        for j in range(tb // ML_KCHUNK):
            o_ref[j] = y[j * ML_KCHUNK:(j + 1) * ML_KCHUNK, :].T.astype(BF16)
    else:
        o_ref[...] = y.astype(BF16)


def _conv_silu(proj, conv_w, conv_b, tb, is_key):
    S = proj.shape[0]
    nblk = S // tb
    hb = tb // HALO
    c = 1 if is_key else 0
    if is_key:
        out_spec = pl.BlockSpec((tb // ML_KCHUNK, ML_WIDTH, ML_KCHUNK), lambda i: (i, 0, 0))
        out_shape = jax.ShapeDtypeStruct((S // ML_KCHUNK, ML_WIDTH, ML_KCHUNK), BF16)
    else:
        out_spec = pl.BlockSpec((tb, ML_WIDTH), lambda i: (i, 0))
        out_shape = jax.ShapeDtypeStruct((S, ML_WIDTH), BF16)
    return pl.pallas_call(
        functools.partial(_conv_kernel, tb=tb, nblk=nblk, is_key=is_key),
        grid=(nblk,),
        in_specs=[
            pl.BlockSpec((tb, COL_BLOCK), lambda i: (i, CB_ML_Q + c)),
            pl.BlockSpec((HALO, COL_BLOCK), lambda i: (jnp.maximum(i * hb - 1, 0), CB_ML_Q + c)),
            pl.BlockSpec((HALO, COL_BLOCK), lambda i: (jnp.minimum((i + 1) * hb, S // HALO - 1), CB_ML_Q + c)),
            pl.BlockSpec((3, COL_BLOCK), lambda i: (0, c)),
            pl.BlockSpec((1, COL_BLOCK), lambda i: (0, c)),
        ],
        out_specs=out_spec,
        out_shape=out_shape,
        compiler_params=_params(("parallel",)),
        name="conv_silu_k" if is_key else "conv_silu_q",
    )(proj, proj, proj, conv_w, conv_b)


ML_AUG = ML_HEAD_DIM + LANES


def _log_sigmoid(x):
    return jnp.minimum(x, 0.0) - jnp.log1p(jnp.exp(-jnp.abs(x)))


def _split3(x):
    x1 = x.astype(BF16)
    r1 = x - x1.astype(F32)
    x2 = r1.astype(BF16)
    x3 = (r1 - x2.astype(F32)).astype(BF16)
    return x1, x2, x3


def _mlstm_kernel(qf_ref, ktf_ref, vf_ref, grf_ref, qb_ref, ktb_ref, vb_ref, grb_ref,
                  hf_ref, hb_ref, c_ref, m_ref, *, G):
    @pl.when(pl.program_id(0) == 0)
    def _():
        c_ref[...] = jnp.zeros_like(c_ref)
        m_ref[...] = jnp.zeros_like(m_ref)

    L = ML_KCHUNK
    ri = lax.broadcasted_iota(jnp.int32, (L, L), 0)
    ci = lax.broadcasted_iota(jnp.int32, (L, L), 1)
    tril = ci <= ri
    triu = ci >= ri
    trilb = tril.astype(BF16)
    triub = triu.astype(BF16)
    ones_blk = jnp.ones((L, LANES), BF16)
    pad_rows = jnp.zeros((LANES - N_ML_GATES, L), F32)

    def chain(dirn, cc):
        if dirn == 0:
            q_ref, kt_ref, v_ref, gr_ref, h_ref = qf_ref, ktf_ref, vf_ref, grf_ref, hf_ref
            tri_m, mask = triub, tril
        else:
            q_ref, kt_ref, v_ref, gr_ref, h_ref = qb_ref, ktb_ref, vb_ref, grb_ref, hb_ref
            tri_m, mask = trilb, triu
        ioff = dirn * 2 * ML_HEADS
        foff = ioff + ML_HEADS
        toff = pl.multiple_of(cc * L, L)
        grow = gr_ref[cc]
        br = jnp.dot(jnp.concatenate(_split3(_log_sigmoid(grow)), axis=0), tri_m, preferred_element_type=F32)
        brow_all = br[:N_ML_GATES] + br[N_ML_GATES:2 * N_ML_GATES] + br[2 * N_ML_GATES:]
        bcol_all = jnp.concatenate([brow_all, pad_rows], axis=0).T
        for h in range(ML_HEADS):
            ch = dirn * ML_HEADS + h
            m = m_ref[ch:ch + 1, 0:1]
            bcol = bcol_all[:, foff + h:foff + h + 1]
            brow = brow_all[foff + h:foff + h + 1, :]
            irow = grow[ioff + h:ioff + h + 1, :]
            log_d = jnp.where(mask, bcol - brow + irow, NEG_BIG)
            m_t = jnp.maximum(bcol + m, jnp.max(log_d, axis=-1, keepdims=True))
            dmat = jnp.exp(log_d - m_t)
            inter = jnp.exp(bcol + m - m_t)
            hs = slice(h * ML_HEAD_DIM, (h + 1) * ML_HEAD_DIM)
            q = q_ref[pl.ds(toff, L), hs]
            kt = kt_ref[cc, hs, :]
            v = v_ref[pl.ds(toff, L), hs]
            vaug = jnp.concatenate([v, ones_blk], axis=1)
            s = jnp.dot(q, kt, preferred_element_type=F32)
            sd = (s * dmat).astype(BF16)
            cst = c_ref[ch]
            num = (jnp.dot(sd, vaug, preferred_element_type=F32)
                   + inter * jnp.dot(q, cst.astype(BF16), preferred_element_type=F32))
            den = num[:, ML_HEAD_DIM:ML_HEAD_DIM + 1]
            h_ref[pl.ds(toff, L), hs] = (num[:, :ML_HEAD_DIM]
                                         / jnp.maximum(jnp.abs(den), jnp.exp(-m_t))).astype(BF16)
            g = brow[:, L - 1:L] if dirn == 0 else brow[:, 0:1]
            log_w = g - brow + irow
            m_new = jnp.maximum(g + m, jnp.max(log_w, axis=-1, keepdims=True))
            w = jnp.exp(log_w - m_new)
            decay = jnp.exp(g + m - m_new)
            kw = (kt.astype(F32) * w).astype(BF16)
            upd = jnp.dot(kw, vaug, preferred_element_type=F32)
            c_ref[ch] = decay * cst + upd
            m_ref[ch:ch + 1, :] = jnp.broadcast_to(m_new, (1, LANES))

    def chunk_body(c, carry):
        chain(0, c)
        chain(1, G - 1 - c)
        return carry

    lax.fori_loop(0, G, chunk_body, 0)


def _mlstm_scan(q, kt, proj, gates_t, G):
    S = q.shape[0]
    tb = G * ML_KCHUNK
    nb = S // tb
    fwd = lambda i: i
    bwd = lambda i: nb - 1 - i
    in_specs = []
    for mp in (fwd, bwd):
        in_specs += [
            pl.BlockSpec((tb, ML_WIDTH), lambda i, mp=mp: (mp(i), 0)),
            pl.BlockSpec((G, ML_WIDTH, ML_KCHUNK), lambda i, mp=mp: (mp(i), 0, 0)),
            pl.BlockSpec((tb, COL_BLOCK), lambda i, mp=mp: (mp(i), CB_ML_V)),
            pl.BlockSpec((G, N_ML_GATES, ML_KCHUNK), lambda i, mp=mp: (mp(i), 0, 0)),
        ]
    return pl.pallas_call(
        functools.partial(_mlstm_kernel, G=G),
        grid=(nb,),
        in_specs=in_specs,
        out_specs=[
            pl.BlockSpec((tb, ML_WIDTH), lambda i: (fwd(i), 0)),
            pl.BlockSpec((tb, ML_WIDTH), lambda i: (bwd(i), 0)),
        ],
        out_shape=[jax.ShapeDtypeStruct((S, ML_WIDTH), BF16), jax.ShapeDtypeStruct((S, ML_WIDTH), BF16)],
        scratch_shapes=[
            pltpu.VMEM((2 * ML_HEADS, ML_HEAD_DIM, ML_AUG), F32),
            pltpu.VMEM((2 * ML_HEADS, LANES), F32),
        ],
        compiler_params=_params(("arbitrary",)),
        name="mlstm_scan",
    )(q, kt, proj, gates_t, q, kt, proj, gates_t)


MERGE_NB = D_MODEL // COL_BLOCK


def _merge_kernel(yna_ref, hf_ref, hb_ref, o_ref, *rest):
    g_refs = rest[:2 * MERGE_NB]
    bias_ref, hg_ref, wna_ref, wml_ref, out_ref, yml_ref = rest[2 * MERGE_NB:]
    for h in range(ML_HEADS):
        sl = slice(h * ML_HEAD_DIM, (h + 1) * ML_HEAD_DIM)
        hh = hf_ref[:, sl].astype(F32) + hb_ref[:, sl].astype(F32)
        ms = jnp.mean(hh * hh, axis=-1, keepdims=True)
        y = (hh * lax.rsqrt(ms + EPS)) * hg_ref[:, sl]
        yml_ref[:, sl] = (y * _sigmoid(o_ref[:, sl].astype(F32))).astype(BF16)
    for n in range(MERGE_NB):
        cols = slice(n * COL_BLOCK, (n + 1) * COL_BLOCK)
        ga = _sigmoid(g_refs[n][...].astype(F32) + bias_ref[:, cols])
        gb = _sigmoid(g_refs[MERGE_NB + n][...].astype(F32) + bias_ref[:, D_MODEL + n * COL_BLOCK:
                                                                       D_MODEL + (n + 1) * COL_BLOCK])
        a = jnp.dot(yna_ref[...], wna_ref[:, cols], preferred_element_type=F32)
        bm = jnp.dot(yml_ref[...], wml_ref[:, cols], preferred_element_type=F32)
        out_ref[:, cols] = (ga * a + gb * bm).astype(BF16)


def _merge(y_na, hf, hb, proj, b_branch, hnorm_g, w_na_out, w_ml_out, tm):
    S = y_na.shape[0]
    const = lambda shape: pl.BlockSpec(shape, lambda i: (0, 0), pipeline_mode=pl.Buffered(1))
    gate_specs = [pl.BlockSpec((tm, COL_BLOCK), lambda i, c=c: (i, CB_BR + c)) for c in range(2 * MERGE_NB)]
    return pl.pallas_call(
        _merge_kernel,
        grid=(S // tm,),
        in_specs=[
            pl.BlockSpec((tm, NA_WIDTH), lambda i: (i, 0)),
            pl.BlockSpec((tm, ML_WIDTH), lambda i: (i, 0)),
            pl.BlockSpec((tm, ML_WIDTH), lambda i: (i, 0)),
            pl.BlockSpec((tm, COL_BLOCK), lambda i: (i, CB_ML_O)),
        ] + gate_specs + [
            const((1, 2 * D_MODEL)),
            const((1, ML_WIDTH)),
            const((NA_WIDTH, D_MODEL)),
            const((ML_WIDTH, D_MODEL)),
        ],
        out_specs=pl.BlockSpec((tm, D_MODEL), lambda i: (i, 0)),
        out_shape=jax.ShapeDtypeStruct((S, D_MODEL), BF16),
        scratch_shapes=[pltpu.VMEM((tm, ML_WIDTH), BF16)],
        compiler_params=_params(("parallel",)),
        name="merge",
    )(y_na, hf, hb, proj, *([proj] * (2 * MERGE_NB)), b_branch, hnorm_g, w_na_out, w_ml_out)


def _outproj_kernel(mg_ref, x_ref, wo_ref, g2_ref, wr_ref, br_ref, x1_ref, xn_ref, ids_ref, wts_ref):
    x1 = x_ref[...] + jnp.dot(mg_ref[...], wo_ref[...], preferred_element_type=F32)
    x1_ref[...] = x1
    ms = jnp.mean(x1 * x1, axis=-1, keepdims=True)
    xn = (x1 * lax.rsqrt(ms + EPS)) * g2_ref[...]
    xn_ref[...] = xn
    xh = xn.astype(BF16)
    xl = (xn - xh.astype(F32)).astype(BF16)
    hl = jnp.dot(xh, wr_ref[...], preferred_element_type=F32)
    logits = (hl[:, :LANES] + hl[:, LANES:]
              + jnp.dot(xl, wr_ref[:, :LANES], preferred_element_type=F32) + br_ref[...])
    tm = logits.shape[0]
    lane = lax.broadcasted_iota(jnp.int32, (tm, LANES), 1)
    big = jnp.int32(1 << 20)
    gmask = lane < N_GROUPS
    lg = jnp.where(gmask, logits, NEG_BIG)
    gmax = jnp.max(lg, axis=-1, keepdims=True)
    p_group = 1.0 / jnp.sum(jnp.exp(lg - gmax), axis=-1, keepdims=True)
    g_idx = jnp.min(jnp.where(lg == gmax, lane, big), axis=-1, keepdims=True)
    e_lo = N_GROUPS + g_idx * EXPERTS_PER_GROUP
    emask = (lane >= e_lo) & (lane < e_lo + EXPERTS_PER_GROUP)
    le = jnp.where(emask, logits, NEG_BIG)
    m1 = jnp.max(le, axis=-1, keepdims=True)
    i1 = jnp.min(jnp.where(le == m1, lane, big), axis=-1, keepdims=True)
    le2 = jnp.where(lane == i1, NEG_BIG, le)
    m2 = jnp.max(le2, axis=-1, keepdims=True)
    i2 = jnp.min(jnp.where(le2 == m2, lane, big), axis=-1, keepdims=True)
    e2 = jnp.exp(m2 - m1)
    w1 = p_group / (1.0 + e2)
    w2 = p_group * e2 / (1.0 + e2)
    ids_ref[...] = jnp.where(lane == 0, i1 - N_GROUPS, jnp.where(lane == 1, i2 - N_GROUPS, 0))
    wts_ref[...] = jnp.where(lane == 0, w1, jnp.where(lane == 1, w2, 0.0))


def _outproj(merged, x, w_o, g2, w_router, b_router, tm):
    S = x.shape[0]
    return pl.pallas_call(
        _outproj_kernel,
        grid=(S // tm,),
        in_specs=[
            pl.BlockSpec((tm, D_MODEL), lambda i: (i, 0)),
            pl.BlockSpec((tm, D_MODEL), lambda i: (i, 0)),
            pl.BlockSpec((D_MODEL, D_MODEL), lambda i: (0, 0), pipeline_mode=pl.Buffered(1)),
            pl.BlockSpec((1, D_MODEL), lambda i: (0, 0)),
            pl.BlockSpec((D_MODEL, 2 * LANES), lambda i: (0, 0)),
            pl.BlockSpec((1, LANES), lambda i: (0, 0)),
        ],
        out_specs=[
            pl.BlockSpec((tm, D_MODEL), lambda i: (i, 0)),
            pl.BlockSpec((tm, D_MODEL), lambda i: (i, 0)),
            pl.BlockSpec((tm, LANES), lambda i: (i, 0)),
            pl.BlockSpec((tm, LANES), lambda i: (i, 0)),
        ],
        out_shape=[
            jax.ShapeDtypeStruct((S, D_MODEL), F32),
            jax.ShapeDtypeStruct((S, D_MODEL), F32),
            jax.ShapeDtypeStruct((S, LANES), jnp.int32),
            jax.ShapeDtypeStruct((S, LANES), F32),
        ],
        compiler_params=_params(("parallel",)),
        name="outproj_router",
    )(merged, x, w_o, g2, w_router, b_router)


MOE_TM = 256


def _route_plan(ids, tm):
    n_tok = ids.shape[0]
    n_pairs = 2 * n_tok
    nt = n_pairs // tm + N_EXPERTS
    assert n_pairs & (n_pairs - 1) == 0 and nt * tm <= (1 << 16), "pair ids / rows are packed into 16-bit fields"
    e = ids.T.reshape(-1)
    pid = jnp.arange(n_pairs, dtype=jnp.int32)
    order = lax.sort(e * n_pairs + pid) & (n_pairs - 1)
    experts = jnp.arange(N_EXPERTS, dtype=jnp.int32)
    cnt = jnp.sum((e[None, :] == experts[:, None]).astype(jnp.int32), axis=1)
    tiles_e = (cnt + tm - 1) // tm
    tend = jnp.cumsum(tiles_e)
    tstart = tend - tiles_e
    cend = jnp.cumsum(cnt)
    cstart = cend - cnt
    tile_idx = jnp.arange(nt, dtype=jnp.int32)
    te = jnp.minimum(jnp.sum((tile_idx[:, None] >= tend[None, :]).astype(jnp.int32), axis=1), N_EXPERTS - 1)
    local = (tile_idx - tstart[te]) * tm
    base = jnp.where(cnt[te] - local > 0, cstart[te] + local, 0)
    off = tstart * tm - cstart
    steps = jnp.where(pid[None, :] >= cend[:-1, None], (off[1:] - off[:-1])[:, None], 0)
    row_of_pos = pid + off[0] + jnp.sum(steps, axis=0)
    dest = lax.sort(order.astype(jnp.uint32) * jnp.uint32(1 << 16) + row_of_pos.astype(jnp.uint32))
    dest = (dest & jnp.uint32(0xFFFF)).astype(jnp.int32)
    return jnp.concatenate([order, pid[:tm]]), te.astype(jnp.int32), base.astype(jnp.int32), tend[-1:], dest


def _moe_tile_kernel(te_ref, base_ref, nact_ref, order_ref, xn_hbm, wg0_ref, wu0_ref, wd0_ref, wg1_ref, wu1_ref,
                     wd1_ref, y_ref, gbuf0, gbuf1, gsem, *, tm, n_tok):
    k = pl.program_id(0)
    n_active = nact_ref[0]
    gbuf = (gbuf0, gbuf1)
    wts = ((wg0_ref, wu0_ref, wd0_ref), (wg1_ref, wu1_ref, wd1_ref))

    def issue_gather(tile, par):
        first = base_ref[tile]
        for r in range(tm):
            tok = order_ref[first + r] & (n_tok - 1)
            pltpu.make_async_copy(xn_hbm.at[pl.ds(tok, 1), :], gbuf[par].at[pl.ds(r, 1), :], gsem.at[par]).start()

    def wait_gather(par):
        pltpu.make_async_copy(xn_hbm.at[pl.ds(0, tm), :], gbuf[par], gsem.at[par]).wait()

    def half_step(tile, par):
        rows = slice(par * tm, (par + 1) * tm)

        @pl.when(tile < n_active)
        def _():
            wait_gather(par)

        def compute():
            wg_ref, wu_ref, wd_ref = wts[par]
            xg = gbuf[par][...].astype(BF16)
            hg = jnp.dot(xg, wg_ref[0], preferred_element_type=F32)
            hu = jnp.dot(xg, wu_ref[0], preferred_element_type=F32)
            hh = ((hg * _sigmoid(hg)) * hu).astype(BF16)
            y_ref[rows, :] = jnp.dot(hh, wd_ref[0], preferred_element_type=F32)

        @pl.when(tile + 1 < n_active)
        def _():
            issue_gather(tile + 1, 1 - par)
            compute()

        @pl.when(tile + 1 == n_active)
        def _():
            compute()

        @pl.when(tile >= n_active)
        def _():
            y_ref[rows, :] = jnp.zeros((tm, D_MODEL), F32)

    @pl.when((k == 0) & (n_active > 0))
    def _():
        issue_gather(0, 0)

    half_step(2 * k, 0)
    half_step(2 * k + 1, 1)


def _moe_tiles(xn2, order, tile_expert, tile_base, n_active, w_gate, w_up, w_down, tm):
    n_tok = xn2.shape[0]
    assert n_tok & (n_tok - 1) == 0, "token count must be a power of two (pair id -> token by masking)"
    nt = tile_expert.shape[0]
    assert nt % 2 == 0

    def wspec(shape, par):
        return pl.BlockSpec(shape, lambda k, te, base, nact, order: (te[2 * k + par], 0, 0))

    w_specs = [wspec(s, par) for par in (0, 1)
               for s in ((1, D_MODEL, D_EXPERT), (1, D_MODEL, D_EXPERT), (1, D_EXPERT, D_MODEL))]
    return pl.pallas_call(
        functools.partial(_moe_tile_kernel, tm=tm, n_tok=n_tok),
        grid_spec=pltpu.PrefetchScalarGridSpec(
            num_scalar_prefetch=4,
            grid=(nt // 2,),
            in_specs=[pl.BlockSpec(memory_space=pl.ANY)] + w_specs,
            out_specs=pl.BlockSpec((2 * tm, D_MODEL), lambda k, te, base, nact, order: (k, 0)),
            scratch_shapes=[
                pltpu.VMEM((tm, D_MODEL), F32),
                pltpu.VMEM((tm, D_MODEL), F32),
                pltpu.SemaphoreType.DMA((2,)),
            ],
        ),
        out_shape=jax.ShapeDtypeStruct((nt * tm, D_MODEL), F32),
        compiler_params=_params(("arbitrary",), disable_bounds_checks=True),
        name="moe_tiles",
    )(tile_expert, tile_base, n_active, order, xn2, w_gate, w_up, w_down, w_gate, w_up, w_down)


def _combine_kernel(dest_ref, x1_ref, wts_ref, ys_hbm, out_ref, ybuf, sem, *, tm, nblk, n_tok):
    i = pl.program_id(0)
    slot = i % 2

    def issue(blk, dst_slot):
        t0 = blk * tm
        for r in range(tm):
            for j in range(2):
                d = dest_ref[j * n_tok + t0 + r]
                pltpu.make_async_copy(ys_hbm.at[pl.ds(d, 1), :], ybuf.at[dst_slot, j, pl.ds(r, 1), :],
                                      sem.at[dst_slot]).start()

    def wait(w_slot):
        for j in range(2):
            pltpu.make_async_copy(ys_hbm.at[pl.ds(0, tm), :], ybuf.at[w_slot, j], sem.at[w_slot]).wait()

    @pl.when(i == 0)
    def _():
        issue(0, 0)

    issue(jnp.minimum(i + 1, nblk - 1), 1 - slot)
    wait(slot)
    w = wts_ref[...]
    out_ref[...] = x1_ref[...] + (w[:, 0:1] * ybuf[slot, 0] + w[:, 1:2] * ybuf[slot, 1])

    @pl.when(i == nblk - 1)
    def _():
        wait(1 - slot)


def _combine(x1, ys, dest, wts, tm):
    S = x1.shape[0]
    nblk = S // tm
    return pl.pallas_call(
        functools.partial(_combine_kernel, tm=tm, nblk=nblk, n_tok=S),
        grid_spec=pltpu.PrefetchScalarGridSpec(
            num_scalar_prefetch=1,
            grid=(nblk,),
            in_specs=[
                pl.BlockSpec((tm, D_MODEL), lambda i, dest: (i, 0)),
                pl.BlockSpec((tm, LANES), lambda i, dest: (i, 0)),
                pl.BlockSpec(memory_space=pl.ANY),
            ],
            out_specs=pl.BlockSpec((tm, D_MODEL), lambda i, dest: (i, 0)),
            scratch_shapes=[
                pltpu.VMEM((2, 2, tm, D_MODEL), F32),
                pltpu.SemaphoreType.DMA((2,)),
            ],
        ),
        out_shape=jax.ShapeDtypeStruct((S, D_MODEL), F32),
        compiler_params=_params(("arbitrary",), disable_bounds_checks=True),
        name="moe_combine",
    )(dest, x1, wts, ys)


def _moe(xn2, x1, ids, wts, w_gate, w_up, w_down):
    tm = MOE_TM
    order, tile_expert, tile_base, n_active, dest = _route_plan(ids[:, :2], tm)
    ys = _moe_tiles(xn2, order, tile_expert, tile_base, n_active, w_gate, w_up, w_down, tm)
    return _combine(x1, ys, dest, wts, _tile(x1.shape[0], MOE_TM))


def _tile(S, want):
    return min(want, S)


def _layer(x, p):
    S = x.shape[0]
    proj, gates = _inproj(x, p["g1"], p["w_head"], p["w_br"], p["w_gate_in"], p["b_gate_in"], p["qk_gain"],
                          _tile(S, 1024))
    y_na = _na_attention(proj, p["na_bias"])
    q_ml = _conv_silu(proj, p["conv_w"], p["conv_b"], _tile(S, 512), is_key=False)
    kt_ml = _conv_silu(proj, p["conv_w"], p["conv_b"], _tile(S, 512), is_key=True)
    G = 2
    gates_t = jnp.transpose(gates[:, :N_ML_GATES].reshape(S // ML_KCHUNK, ML_KCHUNK, N_ML_GATES), (0, 2, 1))
    hf, hb = _mlstm_scan(q_ml, kt_ml, proj, gates_t, G)
    merged = _merge(y_na, hf, hb, proj, p["b_branch"], p["hnorm_g"], p["w_na_out"], p["w_ml_out"], _tile(S, 512))
    x1, xn2, ids, wts = _outproj(merged, x, p["w_o"], p["g2"], p["w_router"], p["b_router"], _tile(S, 512))
    return _moe(xn2, x1, ids, wts, p["w_gate"], p["w_up"], p["w_down"])


def _prepare(l, norm1_g, w_in, b_ml_gates, b_branch, qn_g, kn_g, na_rpb, ml_conv_w, ml_conv_b, ml_hnorm_g,
             w_na_out, w_ml_out, w_o, norm2_g, w_router_group, b_router_group, w_router_expert,
             b_router_expert, w_gate, w_up, w_down):
    g_lo = 3 * NA_WIDTH + 4 * ML_WIDTH
    g_hi = g_lo + N_ML_GATES
    w = w_in[l]
    pad = LANES - N_ML_GATES
    n_r = N_GROUPS + N_EXPERTS
    w_r = jnp.pad(jnp.concatenate([w_router_group[l], w_router_expert[l]], axis=1), ((0, 0), (0, LANES - n_r)))
    w_r_hi = w_r.astype(BF16)
    w_r_lo = (w_r - w_r_hi.astype(F32)).astype(BF16)
    return {
        "g1": norm1_g[l].reshape(1, D_MODEL),
        "w_head": w[:, :g_lo].astype(BF16),
        "w_br": _branch_gate_weights(w),
        "w_gate_in": jnp.pad(w[:, g_lo:g_hi], ((0, 0), (0, pad))).astype(BF16),
        "b_gate_in": jnp.pad(b_ml_gates[l], (0, pad)).reshape(1, LANES),
        "qk_gain": jnp.concatenate([qn_g[l].reshape(-1), kn_g[l].reshape(-1)]).reshape(1, 2 * NA_WIDTH),
        "na_bias": _na_bias_table(na_rpb[l]),
        "conv_w": ml_conv_w[l],
        "conv_b": ml_conv_b[l].reshape(1, 2 * ML_WIDTH),
        "b_branch": b_branch[l].reshape(1, 2 * D_MODEL),
        "hnorm_g": ml_hnorm_g[l].reshape(1, ML_WIDTH),
        "w_na_out": w_na_out[l].astype(BF16),
        "w_ml_out": w_ml_out[l].astype(BF16),
        "w_o": w_o[l].astype(BF16),
        "g2": norm2_g[l].reshape(1, D_MODEL),
        "w_router": jnp.concatenate([w_r_hi, w_r_lo], axis=1),
        "b_router": jnp.pad(jnp.concatenate([b_router_group[l], b_router_expert[l]]), (0, LANES - n_r)).reshape(1, LANES),
        "w_gate": w_gate[l].astype(BF16),
        "w_up": w_up[l].astype(BF16),
        "w_down": w_down[l].astype(BF16),
    }


def kernel(x_prompt, x_sample, norm1_g, w_in, b_ml_gates, b_branch, qn_g, kn_g, na_rpb, ml_conv_w, ml_conv_b,
           ml_hnorm_g, w_na_out, w_ml_out, w_o, norm2_g, w_router_group, b_router_group, w_router_expert,
           b_router_expert, w_gate, w_up, w_down):
    depth = w_in.shape[0]
    layers = [_prepare(l, norm1_g, w_in, b_ml_gates, b_branch, qn_g, kn_g, na_rpb, ml_conv_w, ml_conv_b,
                       ml_hnorm_g, w_na_out, w_ml_out, w_o, norm2_g, w_router_group, b_router_group,
                       w_router_expert, b_router_expert, w_gate, w_up, w_down) for l in range(depth)]

    def trunk(x):
        b, s, d = x.shape
        outs = []
        for bi in range(b):
            h = x.reshape(s, d) if b == 1 else x[bi]
            for p in layers:
                h = _layer(h, p)
            outs.append(h)
        return outs[0].reshape(b, s, d) if b == 1 else jnp.stack(outs)

    return (trunk(x_prompt), trunk(x_sample))
```

```python
import functools

import numpy as np
import jax
import jax.numpy as jnp
from jax import lax
from jax.experimental import pallas as pl
from jax.experimental.pallas import tpu as pltpu

F32 = jnp.float32
BF16 = jnp.bfloat16

D_MODEL = 2048
GRID_W = 64
NA_HEADS = 8
NA_HEAD_DIM = 128
NA_WIDTH = NA_HEADS * NA_HEAD_DIM
NA_ROWS = 8
NA_COLS = 16
ML_HEADS = 4
ML_HEAD_DIM = 256
ML_WIDTH = ML_HEADS * ML_HEAD_DIM
ML_CHUNK = 64
N_ML_GATES = 4 * ML_HEADS
N_GROUPS = 4
EXPERTS_PER_GROUP = 4
N_EXPERTS = N_GROUPS * EXPERTS_PER_GROUP
D_EXPERT = 512
EPS = 1e-6

LANES = 128
COL_BLOCK = 1024
N_MAIN = 3 * NA_WIDTH + 4 * ML_WIDTH + 2 * D_MODEL
CB_NA_Q, CB_NA_K, CB_NA_V, CB_ML_Q, CB_ML_K, CB_ML_V, CB_ML_O, CB_BR = 0, 1, 2, 3, 4, 5, 6, 7
NEG_BIG = -1e30
VMEM_LIMIT = 56 * 1024 * 1024


def _sigmoid(x):
    return 0.5 * jnp.tanh(0.5 * x) + 0.5


def _params(sem, vmem=VMEM_LIMIT, **kw):
    return pltpu.CompilerParams(dimension_semantics=sem, vmem_limit_bytes=vmem, **kw)


def _inproj_kernel(x_ref, g1_ref, wa_ref, wb_ref, wg_ref, bg_ref, qkg_ref, proj_ref, gates_ref, xn_ref):
    j = pl.program_id(1)

    @pl.when(j == 0)
    def _():
        x = x_ref[...]
        ms = jnp.mean(x * x, axis=-1, keepdims=True)
        xn = ((x * lax.rsqrt(ms + EPS)) * g1_ref[...]).astype(BF16)
        xn_ref[...] = xn
        gates_ref[...] = jnp.dot(xn, wg_ref[...], preferred_element_type=F32) + bg_ref[...]

    @pl.when(j <= CB_NA_K)
    def _():
        acc = jnp.dot(xn_ref[...], wa_ref[...], preferred_element_type=F32)
        for h in range(NA_HEADS):
            sl = slice(h * NA_HEAD_DIM, (h + 1) * NA_HEAD_DIM)
            a = acc[:, sl]
            ms = jnp.mean(a * a, axis=-1, keepdims=True)
            proj_ref[:, sl] = ((a * lax.rsqrt(ms + EPS)) * qkg_ref[:, sl]).astype(BF16)

    @pl.when((j > CB_NA_K) & (j < CB_BR))
    def _():
        proj_ref[...] = jnp.dot(xn_ref[...], wa_ref[...], preferred_element_type=F32).astype(BF16)

    @pl.when(j >= CB_BR)
    def _():
        proj_ref[...] = jnp.dot(xn_ref[...], wb_ref[...], preferred_element_type=F32).astype(BF16)


def _inproj(x, g1, w_head, w_br, w_gate, b_gate, qk_gain, tm):
    S = x.shape[0]
    nj = N_MAIN // COL_BLOCK
    return pl.pallas_call(
        _inproj_kernel,
        grid=(S // tm, nj),
        in_specs=[
            pl.BlockSpec((tm, D_MODEL), lambda i, j: (i, 0)),
            pl.BlockSpec((1, D_MODEL), lambda i, j: (0, 0)),
            pl.BlockSpec((D_MODEL, COL_BLOCK), lambda i, j: (0, jnp.minimum(j, CB_BR - 1))),
            pl.BlockSpec((D_MODEL, COL_BLOCK), lambda i, j: (0, jnp.maximum(j - CB_BR, 0))),
            pl.BlockSpec((D_MODEL, LANES), lambda i, j: (0, 0)),
            pl.BlockSpec((1, LANES), lambda i, j: (0, 0)),
            pl.BlockSpec((1, COL_BLOCK), lambda i, j: (0, jnp.minimum(j, CB_NA_K))),
        ],
        out_specs=[
            pl.BlockSpec((tm, COL_BLOCK), lambda i, j: (i, j)),
            pl.BlockSpec((tm, LANES), lambda i, j: (i, 0)),
        ],
        out_shape=[
            jax.ShapeDtypeStruct((S, N_MAIN), BF16),
            jax.ShapeDtypeStruct((S, LANES), F32),
        ],
        scratch_shapes=[pltpu.VMEM((tm, D_MODEL), BF16)],
        compiler_params=_params(("parallel", "arbitrary")),
        name="inproj",
    )(x, g1, w_head, w_br, w_gate, b_gate, qk_gain)


NA_QROWS = 8
NA_QTOK = NA_QROWS * GRID_W
NA_KTOK = NA_ROWS * GRID_W


def _na_bias_table(rpb):
    c = np.arange(GRID_W)
    start_c = np.clip(c - NA_COLS // 2, 0, GRID_W - NA_COLS)
    kc = np.arange(GRID_W)
    mask = (kc[None, :] >= start_c[:, None]) & (kc[None, :] < start_c[:, None] + NA_COLS)
    dc = kc[None, :] - c[:, None] + NA_COLS - 1
    onehot = ((dc[None] == np.arange(2 * NA_COLS - 1)[:, None, None]) & mask[None]).astype(np.float32)
    e = jnp.einsum("hrd,dck->hrck", rpb.astype(F32), onehot, precision=lax.Precision.HIGHEST)
    e = jnp.where(mask[None, None], e, NEG_BIG)
    b = jnp.stack([e[:, NA_ROWS - 1 - d:2 * NA_ROWS - 1 - d] for d in range(NA_ROWS)], axis=0)
    b = jnp.transpose(b, (0, 1, 3, 2, 4))
    return b.reshape(NA_ROWS * NA_HEADS, GRID_W, NA_KTOK)


def _na_kernel(q_ref, kp_ref, kc_ref, kn_ref, vp_ref, vc_ref, vn_ref, bias_ref, o_ref, kw_ref, vw_ref,
               s_ref, p_ref, l_ref, *, rows):
    b = pl.program_id(0)
    kw_ref[0:NA_QTOK] = kp_ref[...]
    kw_ref[NA_QTOK:2 * NA_QTOK] = kc_ref[...]
    kw_ref[2 * NA_QTOK:3 * NA_QTOK] = kn_ref[...]
    vw_ref[0:NA_QTOK] = vp_ref[...]
    vw_ref[NA_QTOK:2 * NA_QTOK] = vc_ref[...]
    vw_ref[2 * NA_QTOK:3 * NA_QTOK] = vn_ref[...]
    scale = NA_HEAD_DIM ** -0.5

    def row_body(rl, carry):
        r = b * NA_QROWS + rl
        rs = jnp.clip(r - NA_ROWS // 2, 0, rows - NA_ROWS)
        d = r - rs
        koff = pl.multiple_of((rs - (b - 1) * NA_QROWS) * GRID_W, GRID_W)
        qoff = pl.multiple_of(rl * GRID_W, GRID_W)
        heads = [slice(h * NA_HEAD_DIM, (h + 1) * NA_HEAD_DIM) for h in range(NA_HEADS)]
        for h, sl in enumerate(heads):
            q = q_ref[pl.ds(qoff, GRID_W), sl]
            k = kw_ref[pl.ds(koff, NA_KTOK), sl]
            s = lax.dot_general(q, k, (((1,), (1,)), ((), ())), preferred_element_type=F32)
            s_ref[h] = s * scale + bias_ref[d * NA_HEADS + h]
        for h in range(NA_HEADS):
            s = s_ref[h]
            p = jnp.exp(s - jnp.max(s, axis=-1, keepdims=True))
            l_ref[h] = jnp.sum(p, axis=-1, keepdims=True)
            p_ref[h] = p.astype(BF16)
        for h, sl in enumerate(heads):
            v = vw_ref[pl.ds(koff, NA_KTOK), sl]
            o = jnp.dot(p_ref[h], v, preferred_element_type=F32)
            o_ref[pl.ds(qoff, GRID_W), sl] = (o / l_ref[h]).astype(BF16)
        return carry

    lax.fori_loop(0, NA_QROWS, row_body, 0)


def _na_attention(proj, bias_tab):
    S = proj.shape[0]
    rows = S // GRID_W
    nb = rows // NA_QROWS
    blk = (NA_QTOK, COL_BLOCK)
    prev = lambda b: jnp.maximum(b - 1, 0)
    nxt = lambda b: jnp.minimum(b + 1, nb - 1)
    return pl.pallas_call(
        functools.partial(_na_kernel, rows=rows),
        grid=(nb,),
        in_specs=[
            pl.BlockSpec(blk, lambda b: (b, CB_NA_Q)),
            pl.BlockSpec(blk, lambda b: (prev(b), CB_NA_K)),
            pl.BlockSpec(blk, lambda b: (b, CB_NA_K)),
            pl.BlockSpec(blk, lambda b: (nxt(b), CB_NA_K)),
            pl.BlockSpec(blk, lambda b: (prev(b), CB_NA_V)),
            pl.BlockSpec(blk, lambda b: (b, CB_NA_V)),
            pl.BlockSpec(blk, lambda b: (nxt(b), CB_NA_V)),
            pl.BlockSpec((NA_ROWS * NA_HEADS, GRID_W, NA_KTOK), lambda b: (0, 0, 0)),
        ],
        out_specs=pl.BlockSpec(blk, lambda b: (b, 0)),
        out_shape=jax.ShapeDtypeStruct((S, NA_WIDTH), BF16),
        scratch_shapes=[
            pltpu.VMEM((3 * NA_QTOK, NA_WIDTH), BF16),
            pltpu.VMEM((3 * NA_QTOK, NA_WIDTH), BF16),
            pltpu.VMEM((NA_HEADS, GRID_W, NA_KTOK), F32),
            pltpu.VMEM((NA_HEADS, GRID_W, NA_KTOK), BF16),
            pltpu.VMEM((NA_HEADS, GRID_W, 1), F32),
        ],
        compiler_params=_params(("parallel",)),
        name="na_attention",
    )(proj, proj, proj, proj, proj, proj, proj, bias_tab)


HALO = 16


ML_KCHUNK = 256


def _conv_kernel(x_ref, xp_ref, xn_ref, w_ref, b_ref, o_ref, *, tb, nblk, is_key):
    i = pl.program_id(0)
    x = x_ref[...].astype(F32)
    prev_row = jnp.where(i > 0, xp_ref[HALO - 1:HALO, :].astype(F32), 0.0)
    next_row = jnp.where(i < nblk - 1, xn_ref[0:1, :].astype(F32), 0.0)
    ridx = lax.broadcasted_iota(jnp.int32, (tb, 1), 0)
    x_prev = jnp.where(ridx == 0, prev_row, pltpu.roll(x, 1, 0))
    x_next = jnp.where(ridx == tb - 1, next_row, pltpu.roll(x, tb - 1, 0))
    y = b_ref[...] + x_prev * w_ref[0:1, :] + x * w_ref[1:2, :] + x_next * w_ref[2:3, :]
    y = y * _sigmoid(y)
    if is_key:
        y = y * (ML_HEAD_DIM ** -0.5)
        for j in range(tb // ML_KCHUNK):
            o_ref[j] = y[j * ML_KCHUNK:(j + 1) * ML_KCHUNK, :].T.astype(BF16)
    else:
        o_ref[...] = y.astype(BF16)


def _conv_silu(proj, conv_w, conv_b, tb, is_key):
    S = proj.shape[0]
    nblk = S // tb
    hb = tb // HALO
    c = 1 if is_key else 0
    if is_key:
        out_spec = pl.BlockSpec((tb // ML_KCHUNK, ML_WIDTH, ML_KCHUNK), lambda i: (i, 0, 0))
        out_shape = jax.ShapeDtypeStruct((S // ML_KCHUNK, ML_WIDTH, ML_KCHUNK), BF16)
    else:
        out_spec = pl.BlockSpec((tb, ML_WIDTH), lambda i: (i, 0))
        out_shape = jax.ShapeDtypeStruct((S, ML_WIDTH), BF16)
    return pl.pallas_call(
        functools.partial(_conv_kernel, tb=tb, nblk=nblk, is_key=is_key),
        grid=(nblk,),
        in_specs=[
            pl.BlockSpec((tb, COL_BLOCK), lambda i: (i, CB_ML_Q + c)),
            pl.BlockSpec((HALO, COL_BLOCK), lambda i: (jnp.maximum(i * hb - 1, 0), CB_ML_Q + c)),
            pl.BlockSpec((HALO, COL_BLOCK), lambda i: (jnp.minimum((i + 1) * hb, S // HALO - 1), CB_ML_Q + c)),
            pl.BlockSpec((3, COL_BLOCK), lambda i: (0, c)),
            pl.BlockSpec((1, COL_BLOCK), lambda i: (0, c)),
        ],
        out_specs=out_spec,
        out_shape=out_shape,
        compiler_params=_params(("parallel",)),
        name="conv_silu_k" if is_key else "conv_silu_q",
    )(proj, proj, proj, conv_w, conv_b)


ML_AUG = ML_HEAD_DIM + LANES


def _log_sigmoid(x):
    return jnp.minimum(x, 0.0) - jnp.log1p(jnp.exp(-jnp.abs(x)))


def _split3(x):
    x1 = x.astype(BF16)
    r1 = x - x1.astype(F32)
    x2 = r1.astype(BF16)
    x3 = (r1 - x2.astype(F32)).astype(BF16)
    return x1, x2, x3


def _mlstm_kernel(qf_ref, ktf_ref, vf_ref, grf_ref, qb_ref, ktb_ref, vb_ref, grb_ref,
                  hf_ref, hb_ref, c_ref, m_ref, *, G):
    @pl.when(pl.program_id(0) == 0)
    def _():
        c_ref[...] = jnp.zeros_like(c_ref)
        m_ref[...] = jnp.zeros_like(m_ref)

    L = ML_KCHUNK
    ri = lax.broadcasted_iota(jnp.int32, (L, L), 0)
    ci = lax.broadcasted_iota(jnp.int32, (L, L), 1)
    tril = ci <= ri
    triu = ci >= ri
    trilb = tril.astype(BF16)
    triub = triu.astype(BF16)
    ones_blk = jnp.ones((L, LANES), BF16)
    pad_rows = jnp.zeros((LANES - N_ML_GATES, L), F32)

    def setup(dirn, cc):
        if dirn == 0:
            q_ref, kt_ref, v_ref, gr_ref, h_ref = qf_ref, ktf_ref, vf_ref, grf_ref, hf_ref
            tri_m, mask = triub, tril
        else:
            q_ref, kt_ref, v_ref, gr_ref, h_ref = qb_ref, ktb_ref, vb_ref, grb_ref, hb_ref
            tri_m, mask = trilb, triu
        grow = gr_ref[cc]
        br = jnp.dot(jnp.concatenate(_split3(_log_sigmoid(grow)), axis=0), tri_m, preferred_element_type=F32)
        brow_all = br[:N_ML_GATES] + br[N_ML_GATES:2 * N_ML_GATES] + br[2 * N_ML_GATES:]
        bcol_all = jnp.concatenate([brow_all, pad_rows], axis=0).T
        return dict(dirn=dirn, cc=cc, toff=pl.multiple_of(cc * L, L), q_ref=q_ref, kt_ref=kt_ref, v_ref=v_ref,
                    h_ref=h_ref, mask=mask, grow=grow, brow_all=brow_all, bcol_all=bcol_all)

    def operands(ctx, h):
        hs = slice(h * ML_HEAD_DIM, (h + 1) * ML_HEAD_DIM)
        q = ctx["q_ref"][pl.ds(ctx["toff"], L), hs]
        kt = ctx["kt_ref"][ctx["cc"], hs, :]
        v = ctx["v_ref"][pl.ds(ctx["toff"], L), hs]
        vaug = jnp.concatenate([v, ones_blk], axis=1)
        return q, kt, vaug

    def gate_rows(ctx, h):
        ioff = ctx["dirn"] * 2 * ML_HEADS
        foff = ioff + ML_HEADS
        brow = ctx["brow_all"][foff + h:foff + h + 1, :]
        irow = ctx["grow"][ioff + h:ioff + h + 1, :]
        bcol = ctx["bcol_all"][:, foff + h:foff + h + 1]
        return brow, irow, bcol

    def scores(ctx, h):
        ch = ctx["dirn"] * ML_HEADS + h
        m = m_ref[ch:ch + 1, 0:1]
        brow, irow, bcol = gate_rows(ctx, h)
        log_d = jnp.where(ctx["mask"], bcol - brow + irow, NEG_BIG)
        m_t = jnp.maximum(bcol + m, jnp.max(log_d, axis=-1, keepdims=True))
        dmat = jnp.exp(log_d - m_t)
        inter = jnp.exp(bcol + m - m_t)
        q, kt, _ = operands(ctx, h)
        s = jnp.dot(q, kt, preferred_element_type=F32)
        return (s * dmat).astype(BF16), inter, m_t

    def outputs(ctx, h, sd, inter, m_t):
        ch = ctx["dirn"] * ML_HEADS + h
        hs = slice(h * ML_HEAD_DIM, (h + 1) * ML_HEAD_DIM)
        q, _, vaug = operands(ctx, h)
        num = (jnp.dot(sd, vaug, preferred_element_type=F32)
               + inter * jnp.dot(q, c_ref[ch].astype(BF16), preferred_element_type=F32))
        den = num[:, ML_HEAD_DIM:ML_HEAD_DIM + 1]
        ctx["h_ref"][pl.ds(ctx["toff"], L), hs] = (num[:, :ML_HEAD_DIM]
                                                   / jnp.maximum(jnp.abs(den), jnp.exp(-m_t))).astype(BF16)

    def state_update(ctx, h):
        ch = ctx["dirn"] * ML_HEADS + h
        m = m_ref[ch:ch + 1, 0:1]
        brow, irow, _ = gate_rows(ctx, h)
        g = brow[:, L - 1:L] if ctx["dirn"] == 0 else brow[:, 0:1]
        log_w = g - brow + irow
        m_new = jnp.maximum(g + m, jnp.max(log_w, axis=-1, keepdims=True))
        w = jnp.exp(log_w - m_new)
        decay = jnp.exp(g + m - m_new)
        _, kt, vaug = operands(ctx, h)
        kw = (kt.astype(F32) * w).astype(BF16)
        upd = jnp.dot(kw, vaug, preferred_element_type=F32)
        c_ref[ch] = decay * c_ref[ch] + upd
        m_ref[ch:ch + 1, :] = jnp.broadcast_to(m_new, (1, LANES))

    def chunk_body(c, carry):
        chains = [(ctx, h) for ctx in (setup(0, c), setup(1, G - 1 - c)) for h in range(ML_HEADS)]
        scored = [scores(ctx, h) for ctx, h in chains]
        for (ctx, h), (sd, inter, m_t) in zip(chains, scored):
            outputs(ctx, h, sd, inter, m_t)
        for ctx, h in chains:
            state_update(ctx, h)
        return carry

    lax.fori_loop(0, G, chunk_body, 0)


def _mlstm_scan(q, kt, proj, gates_t, G):
    S = q.shape[0]
    tb = G * ML_KCHUNK
    nb = S // tb
    fwd = lambda i: i
    bwd = lambda i: nb - 1 - i
    in_specs = []
    for mp in (fwd, bwd):
        in_specs += [
            pl.BlockSpec((tb, ML_WIDTH), lambda i, mp=mp: (mp(i), 0)),
            pl.BlockSpec((G, ML_WIDTH, ML_KCHUNK), lambda i, mp=mp: (mp(i), 0, 0)),
            pl.BlockSpec((tb, COL_BLOCK), lambda i, mp=mp: (mp(i), CB_ML_V)),
            pl.BlockSpec((G, N_ML_GATES, ML_KCHUNK), lambda i, mp=mp: (mp(i), 0, 0)),
        ]
    return pl.pallas_call(
        functools.partial(_mlstm_kernel, G=G),
        grid=(nb,),
        in_specs=in_specs,
        out_specs=[
            pl.BlockSpec((tb, ML_WIDTH), lambda i: (fwd(i), 0)),
            pl.BlockSpec((tb, ML_WIDTH), lambda i: (bwd(i), 0)),
        ],
        out_shape=[jax.ShapeDtypeStruct((S, ML_WIDTH), BF16), jax.ShapeDtypeStruct((S, ML_WIDTH), BF16)],
        scratch_shapes=[
            pltpu.VMEM((2 * ML_HEADS, ML_HEAD_DIM, ML_AUG), F32),
            pltpu.VMEM((2 * ML_HEADS, LANES), F32),
        ],
        compiler_params=_params(("arbitrary",)),
        name="mlstm_scan",
    )(q, kt, proj, gates_t, q, kt, proj, gates_t)


MERGE_NB = D_MODEL // COL_BLOCK


def _merge_kernel(yna_ref, hf_ref, hb_ref, o_ref, *rest):
    g_refs = rest[:2 * MERGE_NB]
    bias_ref, hg_ref, wna_ref, wml_ref, out_ref, yml_ref = rest[2 * MERGE_NB:]
    for h in range(ML_HEADS):
        sl = slice(h * ML_HEAD_DIM, (h + 1) * ML_HEAD_DIM)
        hh = hf_ref[:, sl].astype(F32) + hb_ref[:, sl].astype(F32)
        ms = jnp.mean(hh * hh, axis=-1, keepdims=True)
        y = (hh * lax.rsqrt(ms + EPS)) * hg_ref[:, sl]
        yml_ref[:, sl] = (y * _sigmoid(o_ref[:, sl].astype(F32))).astype(BF16)
    for n in range(MERGE_NB):
        cols = slice(n * COL_BLOCK, (n + 1) * COL_BLOCK)
        ga = _sigmoid(g_refs[n][...].astype(F32) + bias_ref[:, cols])
        gb = _sigmoid(g_refs[MERGE_NB + n][...].astype(F32) + bias_ref[:, D_MODEL + n * COL_BLOCK:
                                                                       D_MODEL + (n + 1) * COL_BLOCK])
        a = jnp.dot(yna_ref[...], wna_ref[:, cols], preferred_element_type=F32)
        bm = jnp.dot(yml_ref[...], wml_ref[:, cols], preferred_element_type=F32)
        out_ref[:, cols] = (ga * a + gb * bm).astype(BF16)


def _merge(y_na, hf, hb, proj, b_branch, hnorm_g, w_na_out, w_ml_out, tm):
    S = y_na.shape[0]
    const = lambda shape: pl.BlockSpec(shape, lambda i: (0, 0), pipeline_mode=pl.Buffered(1))
    gate_specs = [pl.BlockSpec((tm, COL_BLOCK), lambda i, c=c: (i, CB_BR + c)) for c in range(2 * MERGE_NB)]
    return pl.pallas_call(
        _merge_kernel,
        grid=(S // tm,),
        in_specs=[
            pl.BlockSpec((tm, NA_WIDTH), lambda i: (i, 0)),
            pl.BlockSpec((tm, ML_WIDTH), lambda i: (i, 0)),
            pl.BlockSpec((tm, ML_WIDTH), lambda i: (i, 0)),
            pl.BlockSpec((tm, COL_BLOCK), lambda i: (i, CB_ML_O)),
        ] + gate_specs + [
            const((1, 2 * D_MODEL)),
            const((1, ML_WIDTH)),
            const((NA_WIDTH, D_MODEL)),
            const((ML_WIDTH, D_MODEL)),
        ],
        out_specs=pl.BlockSpec((tm, D_MODEL), lambda i: (i, 0)),
        out_shape=jax.ShapeDtypeStruct((S, D_MODEL), BF16),
        scratch_shapes=[pltpu.VMEM((tm, ML_WIDTH), BF16)],
        compiler_params=_params(("parallel",)),
        name="merge",
    )(y_na, hf, hb, proj, *([proj] * (2 * MERGE_NB)), b_branch, hnorm_g, w_na_out, w_ml_out)


def _outproj_kernel(mg_ref, x_ref, wo_ref, g2_ref, wr_ref, br_ref, x1_ref, xn_ref, ids_ref, wts_ref):
    x1 = x_ref[...] + jnp.dot(mg_ref[...], wo_ref[...], preferred_element_type=F32)
    x1_ref[...] = x1
    ms = jnp.mean(x1 * x1, axis=-1, keepdims=True)
    xn = (x1 * lax.rsqrt(ms + EPS)) * g2_ref[...]
    xn_ref[...] = xn
    xh = xn.astype(BF16)
    xl = (xn - xh.astype(F32)).astype(BF16)
    hl = jnp.dot(xh, wr_ref[...], preferred_element_type=F32)
    logits = (hl[:, :LANES] + hl[:, LANES:]
              + jnp.dot(xl, wr_ref[:, :LANES], preferred_element_type=F32) + br_ref[...])
    tm = logits.shape[0]
    lane = lax.broadcasted_iota(jnp.int32, (tm, LANES), 1)
    big = jnp.int32(1 << 20)
    gmask = lane < N_GROUPS
    lg = jnp.where(gmask, logits, NEG_BIG)
    gmax = jnp.max(lg, axis=-1, keepdims=True)
    p_group = 1.0 / jnp.sum(jnp.exp(lg - gmax), axis=-1, keepdims=True)
    g_idx = jnp.min(jnp.where(lg == gmax, lane, big), axis=-1, keepdims=True)
    e_lo = N_GROUPS + g_idx * EXPERTS_PER_GROUP
    emask = (lane >= e_lo) & (lane < e_lo + EXPERTS_PER_GROUP)
    le = jnp.where(emask, logits, NEG_BIG)
    m1 = jnp.max(le, axis=-1, keepdims=True)
    i1 = jnp.min(jnp.where(le == m1, lane, big), axis=-1, keepdims=True)
    le2 = jnp.where(lane == i1, NEG_BIG, le)
    m2 = jnp.max(le2, axis=-1, keepdims=True)
    i2 = jnp.min(jnp.where(le2 == m2, lane, big), axis=-1, keepdims=True)
    e2 = jnp.exp(m2 - m1)
    w1 = p_group / (1.0 + e2)
    w2 = p_group * e2 / (1.0 + e2)
    ids_ref[...] = jnp.where(lane == 0, i1 - N_GROUPS, jnp.where(lane == 1, i2 - N_GROUPS, 0))
    wts_ref[...] = jnp.where(lane == 0, w1, jnp.where(lane == 1, w2, 0.0))


def _outproj(merged, x, w_o, g2, w_router, b_router, tm):
    S = x.shape[0]
    return pl.pallas_call(
        _outproj_kernel,
        grid=(S // tm,),
        in_specs=[
            pl.BlockSpec((tm, D_MODEL), lambda i: (i, 0)),
            pl.BlockSpec((tm, D_MODEL), lambda i: (i, 0)),
            pl.BlockSpec((D_MODEL, D_MODEL), lambda i: (0, 0), pipeline_mode=pl.Buffered(1)),
            pl.BlockSpec((1, D_MODEL), lambda i: (0, 0)),
            pl.BlockSpec((D_MODEL, 2 * LANES), lambda i: (0, 0)),
            pl.BlockSpec((1, LANES), lambda i: (0, 0)),
        ],
        out_specs=[
            pl.BlockSpec((tm, D_MODEL), lambda i: (i, 0)),
            pl.BlockSpec((tm, D_MODEL), lambda i: (i, 0)),
            pl.BlockSpec((tm, LANES), lambda i: (i, 0)),
            pl.BlockSpec((tm, LANES), lambda i: (i, 0)),
        ],
        out_shape=[
            jax.ShapeDtypeStruct((S, D_MODEL), F32),
            jax.ShapeDtypeStruct((S, D_MODEL), F32),
            jax.ShapeDtypeStruct((S, LANES), jnp.int32),
            jax.ShapeDtypeStruct((S, LANES), F32),
        ],
        compiler_params=_params(("parallel",)),
        name="outproj_router",
    )(merged, x, w_o, g2, w_router, b_router)


MOE_TM = 256


def _route_plan(ids, tm):
    n_tok = ids.shape[0]
    n_pairs = 2 * n_tok
    nt = n_pairs // tm + N_EXPERTS
    assert n_pairs & (n_pairs - 1) == 0 and nt * tm <= (1 << 16), "pair ids / rows are packed into 16-bit fields"
    e = ids.T.reshape(-1)
    pid = jnp.arange(n_pairs, dtype=jnp.int32)
    order = lax.sort(e * n_pairs + pid) & (n_pairs - 1)
    experts = jnp.arange(N_EXPERTS, dtype=jnp.int32)
    cnt = jnp.sum((e[None, :] == experts[:, None]).astype(jnp.int32), axis=1)
    tiles_e = (cnt + tm - 1) // tm
    tend = jnp.cumsum(tiles_e)
    tstart = tend - tiles_e
    cend = jnp.cumsum(cnt)
    cstart = cend - cnt
    tile_idx = jnp.arange(nt, dtype=jnp.int32)
    te = jnp.minimum(jnp.sum((tile_idx[:, None] >= tend[None, :]).astype(jnp.int32), axis=1), N_EXPERTS - 1)
    local = (tile_idx - tstart[te]) * tm
    base = jnp.where(cnt[te] - local > 0, cstart[te] + local, 0)
    off = tstart * tm - cstart
    steps = jnp.where(pid[None, :] >= cend[:-1, None], (off[1:] - off[:-1])[:, None], 0)
    row_of_pos = pid + off[0] + jnp.sum(steps, axis=0)
    dest = lax.sort(order.astype(jnp.uint32) * jnp.uint32(1 << 16) + row_of_pos.astype(jnp.uint32))
    dest = (dest & jnp.uint32(0xFFFF)).astype(jnp.int32)
    return jnp.concatenate([order, pid[:tm]]), te.astype(jnp.int32), base.astype(jnp.int32), tend[-1:], dest


def _moe_tile_kernel(te_ref, base_ref, nact_ref, order_ref, xn_hbm, wg0_ref, wu0_ref, wd0_ref, wg1_ref, wu1_ref,
                     wd1_ref, y_ref, gbuf0, gbuf1, gsem, *, tm, n_tok):
    k = pl.program_id(0)
    n_active = nact_ref[0]
    gbuf = (gbuf0, gbuf1)
    wts = ((wg0_ref, wu0_ref, wd0_ref), (wg1_ref, wu1_ref, wd1_ref))

    def issue_gather(tile, par):
        first = base_ref[tile]
        for r in range(tm):
            tok = order_ref[first + r] & (n_tok - 1)
            pltpu.make_async_copy(xn_hbm.at[pl.ds(tok, 1), :], gbuf[par].at[pl.ds(r, 1), :], gsem.at[par]).start()

    def wait_gather(par):
        pltpu.make_async_copy(xn_hbm.at[pl.ds(0, tm), :], gbuf[par], gsem.at[par]).wait()

    def half_step(tile, par):
        rows = slice(par * tm, (par + 1) * tm)

        @pl.when(tile < n_active)
        def _():
            wait_gather(par)

        def compute():
            wg_ref, wu_ref, wd_ref = wts[par]
            xg = gbuf[par][...].astype(BF16)
            hg = jnp.dot(xg, wg_ref[0], preferred_element_type=F32)
            hu = jnp.dot(xg, wu_ref[0], preferred_element_type=F32)
            hh = ((hg * _sigmoid(hg)) * hu).astype(BF16)
            y_ref[rows, :] = jnp.dot(hh, wd_ref[0], preferred_element_type=F32)

        @pl.when(tile + 1 < n_active)
        def _():
            issue_gather(tile + 1, 1 - par)

        @pl.when(tile < n_active)
        def _():
            compute()

        @pl.when(tile >= n_active)
        def _():
            y_ref[rows, :] = jnp.zeros((tm, D_MODEL), F32)

    @pl.when((k == 0) & (n_active > 0))
    def _():
        issue_gather(0, 0)

    half_step(2 * k, 0)
    half_step(2 * k + 1, 1)


def _moe_tiles(xn2, order, tile_expert, tile_base, n_active, w_gate, w_up, w_down, tm):
    n_tok = xn2.shape[0]
    assert n_tok & (n_tok - 1) == 0, "token count must be a power of two (pair id -> token by masking)"
    nt = tile_expert.shape[0]
    assert nt % 2 == 0

    def wspec(shape, par):
        return pl.BlockSpec(shape, lambda k, te, base, nact, order: (te[2 * k + par], 0, 0))

    w_specs = [wspec(s, par) for par in (0, 1)
               for s in ((1, D_MODEL, D_EXPERT), (1, D_MODEL, D_EXPERT), (1, D_EXPERT, D_MODEL))]
    return pl.pallas_call(
        functools.partial(_moe_tile_kernel, tm=tm, n_tok=n_tok),
        grid_spec=pltpu.PrefetchScalarGridSpec(
            num_scalar_prefetch=4,
            grid=(nt // 2,),
            in_specs=[pl.BlockSpec(memory_space=pl.ANY)] + w_specs,
            out_specs=pl.BlockSpec((2 * tm, D_MODEL), lambda k, te, base, nact, order: (k, 0)),
            scratch_shapes=[
                pltpu.VMEM((tm, D_MODEL), F32),
                pltpu.VMEM((tm, D_MODEL), F32),
                pltpu.SemaphoreType.DMA((2,)),
            ],
        ),
        out_shape=jax.ShapeDtypeStruct((nt * tm, D_MODEL), F32),
        compiler_params=_params(("arbitrary",), disable_bounds_checks=True),
        name="moe_tiles",
    )(tile_expert, tile_base, n_active, order, xn2, w_gate, w_up, w_down, w_gate, w_up, w_down)


def _combine_kernel(dest_ref, x1_ref, wts_ref, ys_hbm, out_ref, ya0, yb0, ya1, yb1, sem, *, tm, nblk, n_tok):
    i = pl.program_id(0)
    ybuf = ((ya0, yb0), (ya1, yb1))

    def issue(blk, par):
        t0 = blk * tm
        for r in range(tm):
            for j in range(2):
                d = dest_ref[j * n_tok + t0 + r]
                pltpu.make_async_copy(ys_hbm.at[pl.ds(d, 1), :], ybuf[par][j].at[pl.ds(r, 1), :],
                                      sem.at[par]).start()

    def wait(par):
        for j in range(2):
            pltpu.make_async_copy(ys_hbm.at[pl.ds(0, tm), :], ybuf[par][j], sem.at[par]).wait()

    def combine(par):
        rows = slice(par * tm, (par + 1) * tm)
        w = wts_ref[rows, :]
        out_ref[rows, :] = x1_ref[rows, :] + (w[:, 0:1] * ybuf[par][0][...] + w[:, 1:2] * ybuf[par][1][...])

    @pl.when(i == 0)
    def _():
        issue(0, 0)

    issue(2 * i + 1, 1)
    wait(0)
    combine(0)

    @pl.when(2 * i + 2 < nblk)
    def _():
        issue(2 * i + 2, 0)

    wait(1)
    combine(1)


def _combine(x1, ys, dest, wts, tm):
    S = x1.shape[0]
    nblk = S // tm
    assert nblk % 2 == 0
    return pl.pallas_call(
        functools.partial(_combine_kernel, tm=tm, nblk=nblk, n_tok=S),
        grid_spec=pltpu.PrefetchScalarGridSpec(
            num_scalar_prefetch=1,
            grid=(nblk // 2,),
            in_specs=[
                pl.BlockSpec((2 * tm, D_MODEL), lambda i, dest: (i, 0)),
                pl.BlockSpec((2 * tm, LANES), lambda i, dest: (i, 0)),
                pl.BlockSpec(memory_space=pl.ANY),
            ],
            out_specs=pl.BlockSpec((2 * tm, D_MODEL), lambda i, dest: (i, 0)),
            scratch_shapes=[pltpu.VMEM((tm, D_MODEL), F32)] * 4 + [pltpu.SemaphoreType.DMA((2,))],
        ),
        out_shape=jax.ShapeDtypeStruct((S, D_MODEL), F32),
        compiler_params=_params(("arbitrary",), disable_bounds_checks=True),
        name="moe_combine",
    )(dest, x1, wts, ys)


def _moe(xn2, x1, ids, wts, w_gate, w_up, w_down):
    tm = MOE_TM
    order, tile_expert, tile_base, n_active, dest = _route_plan(ids[:, :2], tm)
    ys = _moe_tiles(xn2, order, tile_expert, tile_base, n_active, w_gate, w_up, w_down, tm)
    return _combine(x1, ys, dest, wts, _tile(x1.shape[0], MOE_TM))


def _tile(S, want):
    return min(want, S)


def _layer(x, p):
    S = x.shape[0]
    proj, gates = _inproj(x, p["g1"], p["w_head"], p["w_br"], p["w_gate_in"], p["b_gate_in"], p["qk_gain"],
                          _tile(S, 1024))
    y_na = _na_attention(proj, p["na_bias"])
    q_ml = _conv_silu(proj, p["conv_w"], p["conv_b"], _tile(S, 512), is_key=False)
    kt_ml = _conv_silu(proj, p["conv_w"], p["conv_b"], _tile(S, 512), is_key=True)
    G = 2
    gates_t = jnp.transpose(gates[:, :N_ML_GATES].reshape(S // ML_KCHUNK, ML_KCHUNK, N_ML_GATES), (0, 2, 1))
    hf, hb = _mlstm_scan(q_ml, kt_ml, proj, gates_t, G)
    merged = _merge(y_na, hf, hb, proj, p["b_branch"], p["hnorm_g"], p["w_na_out"], p["w_ml_out"], _tile(S, 512))
    x1, xn2, ids, wts = _outproj(merged, x, p["w_o"], p["g2"], p["w_router"], p["b_router"], _tile(S, 512))
    return _moe(xn2, x1, ids, wts, p["w_gate"], p["w_up"], p["w_down"])


def _prepare(l, norm1_g, w_in, b_ml_gates, b_branch, qn_g, kn_g, na_rpb, ml_conv_w, ml_conv_b, ml_hnorm_g,
             w_na_out, w_ml_out, w_o, norm2_g, w_router_group, b_router_group, w_router_expert,
             b_router_expert, w_gate, w_up, w_down):
    g_lo = 3 * NA_WIDTH + 4 * ML_WIDTH
    g_hi = g_lo + N_ML_GATES
    w = w_in[l]
    pad = LANES - N_ML_GATES
    n_r = N_GROUPS + N_EXPERTS
    w_r = jnp.pad(jnp.concatenate([w_router_group[l], w_router_expert[l]], axis=1), ((0, 0), (0, LANES - n_r)))
    w_r_hi = w_r.astype(BF16)
    w_r_lo = (w_r - w_r_hi.astype(F32)).astype(BF16)
    return {
        "g1": norm1_g[l].reshape(1, D_MODEL),
        "w_head": w[:, :g_lo].astype(BF16),
        "w_br": w[:, g_hi:].astype(BF16),
        "w_gate_in": jnp.pad(w[:, g_lo:g_hi], ((0, 0), (0, pad))).astype(BF16),
        "b_gate_in": jnp.pad(b_ml_gates[l], (0, pad)).reshape(1, LANES),
        "qk_gain": jnp.concatenate([qn_g[l].reshape(-1), kn_g[l].reshape(-1)]).reshape(1, 2 * NA_WIDTH),
        "na_bias": _na_bias_table(na_rpb[l]),
        "conv_w": ml_conv_w[l],
        "conv_b": ml_conv_b[l].reshape(1, 2 * ML_WIDTH),
        "b_branch": b_branch[l].reshape(1, 2 * D_MODEL),
        "hnorm_g": ml_hnorm_g[l].reshape(1, ML_WIDTH),
        "w_na_out": w_na_out[l].astype(BF16),
        "w_ml_out": w_ml_out[l].astype(BF16),
        "w_o": w_o[l].astype(BF16),
        "g2": norm2_g[l].reshape(1, D_MODEL),
        "w_router": jnp.concatenate([w_r_hi, w_r_lo], axis=1),
        "b_router": jnp.pad(jnp.concatenate([b_router_group[l], b_router_expert[l]]), (0, LANES - n_r)).reshape(1, LANES),
        "w_gate": w_gate[l].astype(BF16),
        "w_up": w_up[l].astype(BF16),
        "w_down": w_down[l].astype(BF16),
    }


def kernel(x_prompt, x_sample, norm1_g, w_in, b_ml_gates, b_branch, qn_g, kn_g, na_rpb, ml_conv_w, ml_conv_b,
           ml_hnorm_g, w_na_out, w_ml_out, w_o, norm2_g, w_router_group, b_router_group, w_router_expert,
           b_router_expert, w_gate, w_up, w_down):
    depth = w_in.shape[0]
    layers = [_prepare(l, norm1_g, w_in, b_ml_gates, b_branch, qn_g, kn_g, na_rpb, ml_conv_w, ml_conv_b,
                       ml_hnorm_g, w_na_out, w_ml_out, w_o, norm2_g, w_router_group, b_router_group,
                       w_router_expert, b_router_expert, w_gate, w_up, w_down) for l in range(depth)]

    def trunk(x):
        b, s, d = x.shape
        outs = []
        for bi in range(b):
            h = x.reshape(s, d) if b == 1 else x[bi]
            for p in layers:
                h = _layer(h, p)
            outs.append(h)
        return outs[0].reshape(b, s, d) if b == 1 else jnp.stack(outs)

    return (trunk(x_prompt), trunk(x_sample))
```

```python
import functools

import numpy as np
import jax
import jax.numpy as jnp
from jax import lax
from jax.experimental import pallas as pl
from jax.experimental.pallas import tpu as pltpu

F32 = jnp.float32
BF16 = jnp.bfloat16

D_MODEL = 2048
GRID_W = 64
NA_HEADS = 8
NA_HEAD_DIM = 128
NA_WIDTH = NA_HEADS * NA_HEAD_DIM
NA_ROWS = 8
NA_COLS = 16
ML_HEADS = 4
ML_HEAD_DIM = 256
ML_WIDTH = ML_HEADS * ML_HEAD_DIM
ML_CHUNK = 64
N_ML_GATES = 4 * ML_HEADS
N_GROUPS = 4
EXPERTS_PER_GROUP = 4
N_EXPERTS = N_GROUPS * EXPERTS_PER_GROUP
D_EXPERT = 512
EPS = 1e-6

LANES = 128
COL_BLOCK = 1024
N_MAIN = 3 * NA_WIDTH + 4 * ML_WIDTH + 2 * D_MODEL
CB_NA_Q, CB_NA_K, CB_NA_V, CB_ML_Q, CB_ML_K, CB_ML_V, CB_ML_O, CB_BR = 0, 1, 2, 3, 4, 5, 6, 7
NEG_BIG = -1e30
VMEM_LIMIT = 56 * 1024 * 1024


def _sigmoid(x):
    return 0.5 * jnp.tanh(0.5 * x) + 0.5


def _params(sem, vmem=VMEM_LIMIT, **kw):
    return pltpu.CompilerParams(dimension_semantics=sem, vmem_limit_bytes=vmem, **kw)


def _inproj_kernel(x_ref, g1_ref, wa_ref, wb_ref, wg_ref, bg_ref, qkg_ref, proj_ref, gates_ref, xn_ref):
    j = pl.program_id(1)

    @pl.when(j == 0)
    def _():
        x = x_ref[...]
        ms = jnp.mean(x * x, axis=-1, keepdims=True)
        xn = ((x * lax.rsqrt(ms + EPS)) * g1_ref[...]).astype(BF16)
        xn_ref[...] = xn
        gates_ref[...] = jnp.dot(xn, wg_ref[...], preferred_element_type=F32) + bg_ref[...]

    @pl.when(j <= CB_NA_K)
    def _():
        acc = jnp.dot(xn_ref[...], wa_ref[...], preferred_element_type=F32)
        for h in range(NA_HEADS):
            sl = slice(h * NA_HEAD_DIM, (h + 1) * NA_HEAD_DIM)
            a = acc[:, sl]
            ms = jnp.mean(a * a, axis=-1, keepdims=True)
            proj_ref[:, sl] = ((a * lax.rsqrt(ms + EPS)) * qkg_ref[:, sl]).astype(BF16)

    @pl.when((j > CB_NA_K) & (j < CB_BR))
    def _():
        proj_ref[...] = jnp.dot(xn_ref[...], wa_ref[...], preferred_element_type=F32).astype(BF16)

    @pl.when(j >= CB_BR)
    def _():
        proj_ref[...] = jnp.dot(xn_ref[...], wb_ref[...], preferred_element_type=F32).astype(BF16)


def _inproj(x, g1, w_head, w_br, w_gate, b_gate, qk_gain, tm):
    S = x.shape[0]
    nj = N_MAIN // COL_BLOCK
    return pl.pallas_call(
        _inproj_kernel,
        grid=(S // tm, nj),
        in_specs=[
            pl.BlockSpec((tm, D_MODEL), lambda i, j: (i, 0)),
            pl.BlockSpec((1, D_MODEL), lambda i, j: (0, 0)),
            pl.BlockSpec((None, D_MODEL, COL_BLOCK), lambda i, j: (jnp.minimum(j, CB_BR - 1), 0, 0)),
            pl.BlockSpec((None, D_MODEL, COL_BLOCK), lambda i, j: (jnp.maximum(j - CB_BR, 0), 0, 0)),
            pl.BlockSpec((D_MODEL, LANES), lambda i, j: (0, 0)),
            pl.BlockSpec((1, LANES), lambda i, j: (0, 0)),
            pl.BlockSpec((1, COL_BLOCK), lambda i, j: (0, jnp.minimum(j, CB_NA_K))),
        ],
        out_specs=[
            pl.BlockSpec((None, tm, COL_BLOCK), lambda i, j: (j, i, 0)),
            pl.BlockSpec((tm, LANES), lambda i, j: (i, 0)),
        ],
        out_shape=[
            jax.ShapeDtypeStruct((nj, S, COL_BLOCK), BF16),
            jax.ShapeDtypeStruct((S, LANES), F32),
        ],
        scratch_shapes=[pltpu.VMEM((tm, D_MODEL), BF16)],
        compiler_params=_params(("parallel", "arbitrary")),
        name="inproj",
    )(x, g1, w_head, w_br, w_gate, b_gate, qk_gain)


NA_QROWS = 8
NA_QTOK = NA_QROWS * GRID_W
NA_KTOK = NA_ROWS * GRID_W


def _na_bias_table(rpb):
    c = np.arange(GRID_W)
    start_c = np.clip(c - NA_COLS // 2, 0, GRID_W - NA_COLS)
    kc = np.arange(GRID_W)
    mask = (kc[None, :] >= start_c[:, None]) & (kc[None, :] < start_c[:, None] + NA_COLS)
    dc = kc[None, :] - c[:, None] + NA_COLS - 1
    onehot = ((dc[None] == np.arange(2 * NA_COLS - 1)[:, None, None]) & mask[None]).astype(np.float32)
    e = jnp.einsum("hrd,dck->hrck", rpb.astype(F32), onehot, precision=lax.Precision.HIGHEST)
    e = jnp.where(mask[None, None], e, NEG_BIG)
    b = jnp.stack([e[:, NA_ROWS - 1 - d:2 * NA_ROWS - 1 - d] for d in range(NA_ROWS)], axis=0)
    b = jnp.transpose(b, (0, 1, 3, 2, 4))
    return b.reshape(NA_ROWS * NA_HEADS, GRID_W, NA_KTOK)


def _na_kernel(q_ref, kp_ref, kc_ref, kn_ref, vp_ref, vc_ref, vn_ref, bias_ref, o_ref, kw_ref, vw_ref,
               s_ref, p_ref, l_ref, *, rows):
    b = pl.program_id(0)
    kw_ref[0:NA_QTOK] = kp_ref[...]
    kw_ref[NA_QTOK:2 * NA_QTOK] = kc_ref[...]
    kw_ref[2 * NA_QTOK:3 * NA_QTOK] = kn_ref[...]
    vw_ref[0:NA_QTOK] = vp_ref[...]
    vw_ref[NA_QTOK:2 * NA_QTOK] = vc_ref[...]
    vw_ref[2 * NA_QTOK:3 * NA_QTOK] = vn_ref[...]
    scale = NA_HEAD_DIM ** -0.5

    def row_body(rl, carry):
        r = b * NA_QROWS + rl
        rs = jnp.clip(r - NA_ROWS // 2, 0, rows - NA_ROWS)
        d = r - rs
        koff = pl.multiple_of((rs - (b - 1) * NA_QROWS) * GRID_W, GRID_W)
        qoff = pl.multiple_of(rl * GRID_W, GRID_W)
        heads = [slice(h * NA_HEAD_DIM, (h + 1) * NA_HEAD_DIM) for h in range(NA_HEADS)]
        for h, sl in enumerate(heads):
            q = q_ref[pl.ds(qoff, GRID_W), sl]
            k = kw_ref[pl.ds(koff, NA_KTOK), sl]
            s = lax.dot_general(q, k, (((1,), (1,)), ((), ())), preferred_element_type=F32)
            s_ref[h] = s * scale + bias_ref[d * NA_HEADS + h]
        for h in range(NA_HEADS):
            s = s_ref[h]
            p = jnp.exp(s - jnp.max(s, axis=-1, keepdims=True))
            l_ref[h] = jnp.sum(p, axis=-1, keepdims=True)
            p_ref[h] = p.astype(BF16)
        for h, sl in enumerate(heads):
            v = vw_ref[pl.ds(koff, NA_KTOK), sl]
            o = jnp.dot(p_ref[h], v, preferred_element_type=F32)
            o_ref[pl.ds(qoff, GRID_W), sl] = (o / l_ref[h]).astype(BF16)
        return carry

    lax.fori_loop(0, NA_QROWS, row_body, 0)


def _na_attention(proj, bias_tab):
    S = proj.shape[1]
    rows = S // GRID_W
    nb = rows // NA_QROWS
    blk = (None, NA_QTOK, COL_BLOCK)
    prev = lambda b: jnp.maximum(b - 1, 0)
    nxt = lambda b: jnp.minimum(b + 1, nb - 1)
    return pl.pallas_call(
        functools.partial(_na_kernel, rows=rows),
        grid=(nb,),
        in_specs=[
            pl.BlockSpec(blk, lambda b: (CB_NA_Q, b, 0)),
            pl.BlockSpec(blk, lambda b: (CB_NA_K, prev(b), 0)),
            pl.BlockSpec(blk, lambda b: (CB_NA_K, b, 0)),
            pl.BlockSpec(blk, lambda b: (CB_NA_K, nxt(b), 0)),
            pl.BlockSpec(blk, lambda b: (CB_NA_V, prev(b), 0)),
            pl.BlockSpec(blk, lambda b: (CB_NA_V, b, 0)),
            pl.BlockSpec(blk, lambda b: (CB_NA_V, nxt(b), 0)),
            pl.BlockSpec((NA_ROWS * NA_HEADS, GRID_W, NA_KTOK), lambda b: (0, 0, 0)),
        ],
        out_specs=pl.BlockSpec((NA_QTOK, NA_WIDTH), lambda b: (b, 0)),
        out_shape=jax.ShapeDtypeStruct((S, NA_WIDTH), BF16),
        scratch_shapes=[
            pltpu.VMEM((3 * NA_QTOK, NA_WIDTH), BF16),
            pltpu.VMEM((3 * NA_QTOK, NA_WIDTH), BF16),
            pltpu.VMEM((NA_HEADS, GRID_W, NA_KTOK), F32),
            pltpu.VMEM((NA_HEADS, GRID_W, NA_KTOK), BF16),
            pltpu.VMEM((NA_HEADS, GRID_W, 1), F32),
        ],
        compiler_params=_params(("parallel",)),
        name="na_attention",
    )(proj, proj, proj, proj, proj, proj, proj, bias_tab)


HALO = 16


ML_KCHUNK = 256


def _conv_kernel(x_ref, xp_ref, xn_ref, w_ref, b_ref, o_ref, *, tb, nblk, is_key):
    i = pl.program_id(0)
    x = x_ref[...].astype(F32)
    prev_row = jnp.where(i > 0, xp_ref[HALO - 1:HALO, :].astype(F32), 0.0)
    next_row = jnp.where(i < nblk - 1, xn_ref[0:1, :].astype(F32), 0.0)
    ridx = lax.broadcasted_iota(jnp.int32, (tb, 1), 0)
    x_prev = jnp.where(ridx == 0, prev_row, pltpu.roll(x, 1, 0))
    x_next = jnp.where(ridx == tb - 1, next_row, pltpu.roll(x, tb - 1, 0))
    y = b_ref[...] + x_prev * w_ref[0:1, :] + x * w_ref[1:2, :] + x_next * w_ref[2:3, :]
    y = y * _sigmoid(y)
    if is_key:
        y = y * (ML_HEAD_DIM ** -0.5)
        for j in range(tb // ML_KCHUNK):
            o_ref[j] = y[j * ML_KCHUNK:(j + 1) * ML_KCHUNK, :].T.astype(BF16)
    else:
        o_ref[...] = y.astype(BF16)


def _conv_silu(proj, conv_w, conv_b, tb, is_key):
    S = proj.shape[1]
    nblk = S // tb
    hb = tb // HALO
    c = 1 if is_key else 0
    cb = CB_ML_Q + c
    if is_key:
        out_spec = pl.BlockSpec((tb // ML_KCHUNK, ML_WIDTH, ML_KCHUNK), lambda i: (i, 0, 0))
        out_shape = jax.ShapeDtypeStruct((S // ML_KCHUNK, ML_WIDTH, ML_KCHUNK), BF16)
    else:
        out_spec = pl.BlockSpec((tb, ML_WIDTH), lambda i: (i, 0))
        out_shape = jax.ShapeDtypeStruct((S, ML_WIDTH), BF16)
    return pl.pallas_call(
        functools.partial(_conv_kernel, tb=tb, nblk=nblk, is_key=is_key),
        grid=(nblk,),
        in_specs=[
            pl.BlockSpec((None, tb, COL_BLOCK), lambda i: (cb, i, 0)),
            pl.BlockSpec((None, HALO, COL_BLOCK), lambda i: (cb, jnp.maximum(i * hb - 1, 0), 0)),
            pl.BlockSpec((None, HALO, COL_BLOCK), lambda i: (cb, jnp.minimum((i + 1) * hb, S // HALO - 1), 0)),
            pl.BlockSpec((3, COL_BLOCK), lambda i: (0, c)),
            pl.BlockSpec((1, COL_BLOCK), lambda i: (0, c)),
        ],
        out_specs=out_spec,
        out_shape=out_shape,
        compiler_params=_params(("parallel",)),
        name="conv_silu_k" if is_key else "conv_silu_q",
    )(proj, proj, proj, conv_w, conv_b)


ML_AUG = ML_HEAD_DIM + LANES


def _log_sigmoid(x):
    return jnp.minimum(x, 0.0) - jnp.log1p(jnp.exp(-jnp.abs(x)))


def _split3(x):
    x1 = x.astype(BF16)
    r1 = x - x1.astype(F32)
    x2 = r1.astype(BF16)
    x3 = (r1 - x2.astype(F32)).astype(BF16)
    return x1, x2, x3


def _mlstm_kernel(qf_ref, ktf_ref, vf_ref, grf_ref, qb_ref, ktb_ref, vb_ref, grb_ref,
                  hf_ref, hb_ref, c_ref, m_ref, *, G):
    @pl.when(pl.program_id(0) == 0)
    def _():
        c_ref[...] = jnp.zeros_like(c_ref)
        m_ref[...] = jnp.zeros_like(m_ref)

    L = ML_KCHUNK
    ri = lax.broadcasted_iota(jnp.int32, (L, L), 0)
    ci = lax.broadcasted_iota(jnp.int32, (L, L), 1)
    tril = ci <= ri
    triu = ci >= ri
    trilb = tril.astype(BF16)
    triub = triu.astype(BF16)
    ones_blk = jnp.ones((L, LANES), BF16)
    pad_rows = jnp.zeros((LANES - N_ML_GATES, L), F32)

    def setup(dirn, cc):
        if dirn == 0:
            q_ref, kt_ref, v_ref, gr_ref, h_ref = qf_ref, ktf_ref, vf_ref, grf_ref, hf_ref
            tri_m, mask = triub, tril
        else:
            q_ref, kt_ref, v_ref, gr_ref, h_ref = qb_ref, ktb_ref, vb_ref, grb_ref, hb_ref
            tri_m, mask = trilb, triu
        grow = gr_ref[cc]
        br = jnp.dot(jnp.concatenate(_split3(_log_sigmoid(grow)), axis=0), tri_m, preferred_element_type=F32)
        brow_all = br[:N_ML_GATES] + br[N_ML_GATES:2 * N_ML_GATES] + br[2 * N_ML_GATES:]
        bcol_all = jnp.concatenate([brow_all, pad_rows], axis=0).T
        return dict(dirn=dirn, cc=cc, toff=pl.multiple_of(cc * L, L), q_ref=q_ref, kt_ref=kt_ref, v_ref=v_ref,
                    h_ref=h_ref, mask=mask, grow=grow, brow_all=brow_all, bcol_all=bcol_all)

    def operands(ctx, h):
        hs = slice(h * ML_HEAD_DIM, (h + 1) * ML_HEAD_DIM)
        q = ctx["q_ref"][pl.ds(ctx["toff"], L), hs]
        kt = ctx["kt_ref"][ctx["cc"], hs, :]
        v = ctx["v_ref"][pl.ds(ctx["toff"], L), hs]
        vaug = jnp.concatenate([v, ones_blk], axis=1)
        return q, kt, vaug

    def gate_rows(ctx, h):
        ioff = ctx["dirn"] * 2 * ML_HEADS
        foff = ioff + ML_HEADS
        brow = ctx["brow_all"][foff + h:foff + h + 1, :]
        irow = ctx["grow"][ioff + h:ioff + h + 1, :]
        bcol = ctx["bcol_all"][:, foff + h:foff + h + 1]
        return brow, irow, bcol

    def scores(ctx, h):
        ch = ctx["dirn"] * ML_HEADS + h
        m = m_ref[ch:ch + 1, 0:1]
        brow, irow, bcol = gate_rows(ctx, h)
        log_d = jnp.where(ctx["mask"], bcol - brow + irow, NEG_BIG)
        m_t = jnp.maximum(bcol + m, jnp.max(log_d, axis=-1, keepdims=True))
        dmat = jnp.exp(log_d - m_t)
        inter = jnp.exp(bcol + m - m_t)
        q, kt, _ = operands(ctx, h)
        s = jnp.dot(q, kt, preferred_element_type=F32)
        return (s * dmat).astype(BF16), inter, m_t

    def outputs(ctx, h, sd, inter, m_t):
        ch = ctx["dirn"] * ML_HEADS + h
        hs = slice(h * ML_HEAD_DIM, (h + 1) * ML_HEAD_DIM)
        q, _, vaug = operands(ctx, h)
        num = (jnp.dot(sd, vaug, preferred_element_type=F32)
               + inter * jnp.dot(q, c_ref[ch].astype(BF16), preferred_element_type=F32))
        den = num[:, ML_HEAD_DIM:ML_HEAD_DIM + 1]
        ctx["h_ref"][pl.ds(ctx["toff"], L), hs] = (num[:, :ML_HEAD_DIM]
                                                   / jnp.maximum(jnp.abs(den), jnp.exp(-m_t))).astype(BF16)

    def state_update(ctx, h):
        ch = ctx["dirn"] * ML_HEADS + h
        m = m_ref[ch:ch + 1, 0:1]
        brow, irow, _ = gate_rows(ctx, h)
        g = brow[:, L - 1:L] if ctx["dirn"] == 0 else brow[:, 0:1]
        log_w = g - brow + irow
        m_new = jnp.maximum(g + m, jnp.max(log_w, axis=-1, keepdims=True))
        w = jnp.exp(log_w - m_new)
        decay = jnp.exp(g + m - m_new)
        _, kt, vaug = operands(ctx, h)
        kw = (kt.astype(F32) * w).astype(BF16)
        upd = jnp.dot(kw, vaug, preferred_element_type=F32)
        c_ref[ch] = decay * c_ref[ch] + upd
        m_ref[ch:ch + 1, :] = jnp.broadcast_to(m_new, (1, LANES))

    def chunk_body(c, carry):
        chains = [(ctx, h) for ctx in (setup(0, c), setup(1, G - 1 - c)) for h in range(ML_HEADS)]
        scored = [scores(ctx, h) for ctx, h in chains]
        for (ctx, h), (sd, inter, m_t) in zip(chains, scored):
            outputs(ctx, h, sd, inter, m_t)
        for ctx, h in chains:
            state_update(ctx, h)
        return carry

    lax.fori_loop(0, G, chunk_body, 0)


def _mlstm_scan(q, kt, proj, gates_t, G):
    S = q.shape[0]
    tb = G * ML_KCHUNK
    nb = S // tb
    fwd = lambda i: i
    bwd = lambda i: nb - 1 - i
    in_specs = []
    for mp in (fwd, bwd):
        in_specs += [
            pl.BlockSpec((tb, ML_WIDTH), lambda i, mp=mp: (mp(i), 0)),
            pl.BlockSpec((G, ML_WIDTH, ML_KCHUNK), lambda i, mp=mp: (mp(i), 0, 0)),
            pl.BlockSpec((None, tb, COL_BLOCK), lambda i, mp=mp: (CB_ML_V, mp(i), 0)),
            pl.BlockSpec((G, N_ML_GATES, ML_KCHUNK), lambda i, mp=mp: (mp(i), 0, 0)),
        ]
    return pl.pallas_call(
        functools.partial(_mlstm_kernel, G=G),
        grid=(nb,),
        in_specs=in_specs,
        out_specs=[
            pl.BlockSpec((tb, ML_WIDTH), lambda i: (fwd(i), 0)),
            pl.BlockSpec((tb, ML_WIDTH), lambda i: (bwd(i), 0)),
        ],
        out_shape=[jax.ShapeDtypeStruct((S, ML_WIDTH), BF16), jax.ShapeDtypeStruct((S, ML_WIDTH), BF16)],
        scratch_shapes=[
            pltpu.VMEM((2 * ML_HEADS, ML_HEAD_DIM, ML_AUG), F32),
            pltpu.VMEM((2 * ML_HEADS, LANES), F32),
        ],
        compiler_params=_params(("arbitrary",)),
        name="mlstm_scan",
    )(q, kt, proj, gates_t, q, kt, proj, gates_t)


MERGE_NB = D_MODEL // COL_BLOCK


def _merge_kernel(yna_ref, hf_ref, hb_ref, o_ref, *rest):
    g_refs = rest[:2 * MERGE_NB]
    bias_ref, hg_ref, wna_ref, wml_ref, out_ref, yml_ref = rest[2 * MERGE_NB:]
    for h in range(ML_HEADS):
        sl = slice(h * ML_HEAD_DIM, (h + 1) * ML_HEAD_DIM)
        hh = hf_ref[:, sl].astype(F32) + hb_ref[:, sl].astype(F32)
        ms = jnp.mean(hh * hh, axis=-1, keepdims=True)
        y = (hh * lax.rsqrt(ms + EPS)) * hg_ref[:, sl]
        yml_ref[:, sl] = (y * _sigmoid(o_ref[:, sl].astype(F32))).astype(BF16)
    for n in range(MERGE_NB):
        cols = slice(n * COL_BLOCK, (n + 1) * COL_BLOCK)
        ga = _sigmoid(g_refs[n][...].astype(F32) + bias_ref[:, cols])
        gb = _sigmoid(g_refs[MERGE_NB + n][...].astype(F32) + bias_ref[:, D_MODEL + n * COL_BLOCK:
                                                                       D_MODEL + (n + 1) * COL_BLOCK])
        a = jnp.dot(yna_ref[...], wna_ref[:, cols], preferred_element_type=F32)
        bm = jnp.dot(yml_ref[...], wml_ref[:, cols], preferred_element_type=F32)
        out_ref[:, cols] = (ga * a + gb * bm).astype(BF16)


def _merge(y_na, hf, hb, proj, b_branch, hnorm_g, w_na_out, w_ml_out, tm):
    S = y_na.shape[0]
    const = lambda shape: pl.BlockSpec(shape, lambda i: (0, 0), pipeline_mode=pl.Buffered(1))
    gate_specs = [pl.BlockSpec((None, tm, COL_BLOCK), lambda i, c=c: (CB_BR + c, i, 0)) for c in range(2 * MERGE_NB)]
    return pl.pallas_call(
        _merge_kernel,
        grid=(S // tm,),
        in_specs=[
            pl.BlockSpec((tm, NA_WIDTH), lambda i: (i, 0)),
            pl.BlockSpec((tm, ML_WIDTH), lambda i: (i, 0)),
            pl.BlockSpec((tm, ML_WIDTH), lambda i: (i, 0)),
            pl.BlockSpec((None, tm, COL_BLOCK), lambda i: (CB_ML_O, i, 0)),
        ] + gate_specs + [
            const((1, 2 * D_MODEL)),
            const((1, ML_WIDTH)),
            const((NA_WIDTH, D_MODEL)),
            const((ML_WIDTH, D_MODEL)),
        ],
        out_specs=pl.BlockSpec((tm, D_MODEL), lambda i: (i, 0)),
        out_shape=jax.ShapeDtypeStruct((S, D_MODEL), BF16),
        scratch_shapes=[pltpu.VMEM((tm, ML_WIDTH), BF16)],
        compiler_params=_params(("parallel",)),
        name="merge",
    )(y_na, hf, hb, proj, *([proj] * (2 * MERGE_NB)), b_branch, hnorm_g, w_na_out, w_ml_out)


def _outproj_kernel(mg_ref, x_ref, wo_ref, g2_ref, wr_ref, br_ref, x1_ref, xn_ref, ids_ref, wts_ref):
    x1 = x_ref[...] + jnp.dot(mg_ref[...], wo_ref[...], preferred_element_type=F32)
    x1_ref[...] = x1
    ms = jnp.mean(x1 * x1, axis=-1, keepdims=True)
    xn = (x1 * lax.rsqrt(ms + EPS)) * g2_ref[...]
    xn_ref[...] = xn
    xh = xn.astype(BF16)
    xl = (xn - xh.astype(F32)).astype(BF16)
    hl = jnp.dot(xh, wr_ref[...], preferred_element_type=F32)
    logits = (hl[:, :LANES] + hl[:, LANES:]
              + jnp.dot(xl, wr_ref[:, :LANES], preferred_element_type=F32) + br_ref[...])
    tm = logits.shape[0]
    lane = lax.broadcasted_iota(jnp.int32, (tm, LANES), 1)
    big = jnp.int32(1 << 20)
    gmask = lane < N_GROUPS
    lg = jnp.where(gmask, logits, NEG_BIG)
    gmax = jnp.max(lg, axis=-1, keepdims=True)
    p_group = 1.0 / jnp.sum(jnp.exp(lg - gmax), axis=-1, keepdims=True)
    g_idx = jnp.min(jnp.where(lg == gmax, lane, big), axis=-1, keepdims=True)
    e_lo = N_GROUPS + g_idx * EXPERTS_PER_GROUP
    emask = (lane >= e_lo) & (lane < e_lo + EXPERTS_PER_GROUP)
    le = jnp.where(emask, logits, NEG_BIG)
    m1 = jnp.max(le, axis=-1, keepdims=True)
    i1 = jnp.min(jnp.where(le == m1, lane, big), axis=-1, keepdims=True)
    le2 = jnp.where(lane == i1, NEG_BIG, le)
    m2 = jnp.max(le2, axis=-1, keepdims=True)
    i2 = jnp.min(jnp.where(le2 == m2, lane, big), axis=-1, keepdims=True)
    e2 = jnp.exp(m2 - m1)
    w1 = p_group / (1.0 + e2)
    w2 = p_group * e2 / (1.0 + e2)
    ids_ref[...] = jnp.where(lane == 0, i1 - N_GROUPS, jnp.where(lane == 1, i2 - N_GROUPS, 0))
    wts_ref[...] = jnp.where(lane == 0, w1, jnp.where(lane == 1, w2, 0.0))


def _outproj(merged, x, w_o, g2, w_router, b_router, tm):
    S = x.shape[0]
    return pl.pallas_call(
        _outproj_kernel,
        grid=(S // tm,),
        in_specs=[
            pl.BlockSpec((tm, D_MODEL), lambda i: (i, 0)),
            pl.BlockSpec((tm, D_MODEL), lambda i: (i, 0)),
            pl.BlockSpec((D_MODEL, D_MODEL), lambda i: (0, 0), pipeline_mode=pl.Buffered(1)),
            pl.BlockSpec((1, D_MODEL), lambda i: (0, 0)),
            pl.BlockSpec((D_MODEL, 2 * LANES), lambda i: (0, 0)),
            pl.BlockSpec((1, LANES), lambda i: (0, 0)),
        ],
        out_specs=[
            pl.BlockSpec((tm, D_MODEL), lambda i: (i, 0)),
            pl.BlockSpec((tm, D_MODEL), lambda i: (i, 0)),
            pl.BlockSpec((tm, LANES), lambda i: (i, 0)),
            pl.BlockSpec((tm, LANES), lambda i: (i, 0)),
        ],
        out_shape=[
            jax.ShapeDtypeStruct((S, D_MODEL), F32),
            jax.ShapeDtypeStruct((S, D_MODEL), F32),
            jax.ShapeDtypeStruct((S, LANES), jnp.int32),
            jax.ShapeDtypeStruct((S, LANES), F32),
        ],
        compiler_params=_params(("parallel",)),
        name="outproj_router",
    )(merged, x, w_o, g2, w_router, b_router)


MOE_TM = 256


def _route_plan(ids, tm):
    n_tok = ids.shape[0]
    n_pairs = 2 * n_tok
    nt = n_pairs // tm + N_EXPERTS
    assert n_pairs & (n_pairs - 1) == 0 and nt * tm <= (1 << 16), "pair ids / rows are packed into 16-bit fields"
    e = ids.T.reshape(-1)
    pid = jnp.arange(n_pairs, dtype=jnp.int32)
    order = lax.sort(e * n_pairs + pid) & (n_pairs - 1)
    experts = jnp.arange(N_EXPERTS, dtype=jnp.int32)
    cnt = jnp.sum((e[None, :] == experts[:, None]).astype(jnp.int32), axis=1)
    tiles_e = (cnt + tm - 1) // tm
    tend = jnp.cumsum(tiles_e)
    tstart = tend - tiles_e
    cend = jnp.cumsum(cnt)
    cstart = cend - cnt
    tile_idx = jnp.arange(nt, dtype=jnp.int32)
    te = jnp.minimum(jnp.sum((tile_idx[:, None] >= tend[None, :]).astype(jnp.int32), axis=1), N_EXPERTS - 1)
    local = (tile_idx - tstart[te]) * tm
    base = jnp.where(cnt[te] - local > 0, cstart[te] + local, 0)
    off = tstart * tm - cstart
    steps = jnp.where(pid[None, :] >= cend[:-1, None], (off[1:] - off[:-1])[:, None], 0)
    row_of_pos = pid + off[0] + jnp.sum(steps, axis=0)
    dest = lax.sort(order.astype(jnp.uint32) * jnp.uint32(1 << 16) + row_of_pos.astype(jnp.uint32))
    dest = (dest & jnp.uint32(0xFFFF)).astype(jnp.int32)
    return jnp.concatenate([order, pid[:tm]]), te.astype(jnp.int32), base.astype(jnp.int32), tend[-1:], dest


def _moe_tile_kernel(te_ref, base_ref, nact_ref, order_ref, xn_hbm, wg0_ref, wu0_ref, wd0_ref, wg1_ref, wu1_ref,
                     wd1_ref, y_ref, gbuf0, gbuf1, gsem, *, tm, n_tok):
    k = pl.program_id(0)
    n_active = nact_ref[0]
    gbuf = (gbuf0, gbuf1)
    wts = ((wg0_ref, wu0_ref, wd0_ref), (wg1_ref, wu1_ref, wd1_ref))

    def issue_gather(tile, par):
        first = base_ref[tile]
        for r in range(tm):
            tok = order_ref[first + r] & (n_tok - 1)
            pltpu.make_async_copy(xn_hbm.at[pl.ds(tok, 1), :], gbuf[par].at[pl.ds(r, 1), :], gsem.at[par]).start()

    def wait_gather(par):
        pltpu.make_async_copy(xn_hbm.at[pl.ds(0, tm), :], gbuf[par], gsem.at[par]).wait()

    def half_step(tile, par):
        rows = slice(par * tm, (par + 1) * tm)

        @pl.when(tile < n_active)
        def _():
            wait_gather(par)

        def compute():
            wg_ref, wu_ref, wd_ref = wts[par]
            xg = gbuf[par][...].astype(BF16)
            hg = jnp.dot(xg, wg_ref[0], preferred_element_type=F32)
            hu = jnp.dot(xg, wu_ref[0], preferred_element_type=F32)
            hh = ((hg * _sigmoid(hg)) * hu).astype(BF16)
            y_ref[rows, :] = jnp.dot(hh, wd_ref[0], preferred_element_type=F32)

        @pl.when(tile + 1 < n_active)
        def _():
            issue_gather(tile + 1, 1 - par)

        @pl.when(tile < n_active)
        def _():
            compute()

        @pl.when(tile >= n_active)
        def _():
            y_ref[rows, :] = jnp.zeros((tm, D_MODEL), F32)

    @pl.when((k == 0) & (n_active > 0))
    def _():
        issue_gather(0, 0)

    half_step(2 * k, 0)
    half_step(2 * k + 1, 1)


def _moe_tiles(xn2, order, tile_expert, tile_base, n_active, w_gate, w_up, w_down, tm):
    n_tok = xn2.shape[0]
    assert n_tok & (n_tok - 1) == 0, "token count must be a power of two (pair id -> token by masking)"
    nt = tile_expert.shape[0]
    assert nt % 2 == 0

    def wspec(shape, par):
        return pl.BlockSpec(shape, lambda k, te, base, nact, order: (te[2 * k + par], 0, 0))

    w_specs = [wspec(s, par) for par in (0, 1)
               for s in ((1, D_MODEL, D_EXPERT), (1, D_MODEL, D_EXPERT), (1, D_EXPERT, D_MODEL))]
    return pl.pallas_call(
        functools.partial(_moe_tile_kernel, tm=tm, n_tok=n_tok),
        grid_spec=pltpu.PrefetchScalarGridSpec(
            num_scalar_prefetch=4,
            grid=(nt // 2,),
            in_specs=[pl.BlockSpec(memory_space=pl.ANY)] + w_specs,
            out_specs=pl.BlockSpec((2 * tm, D_MODEL), lambda k, te, base, nact, order: (k, 0)),
            scratch_shapes=[
                pltpu.VMEM((tm, D_MODEL), F32),
                pltpu.VMEM((tm, D_MODEL), F32),
                pltpu.SemaphoreType.DMA((2,)),
            ],
        ),
        out_shape=jax.ShapeDtypeStruct((nt * tm, D_MODEL), F32),
        compiler_params=_params(("arbitrary",), disable_bounds_checks=True),
        name="moe_tiles",
    )(tile_expert, tile_base, n_active, order, xn2, w_gate, w_up, w_down, w_gate, w_up, w_down)


def _combine_kernel(dest_ref, x1_ref, wts_ref, ys_hbm, out_ref, ya0, yb0, ya1, yb1, sem, *, tm, nblk, n_tok):
    i = pl.program_id(0)
    ybuf = ((ya0, yb0), (ya1, yb1))

    def issue(blk, par):
        t0 = blk * tm
        for r in range(tm):
            for j in range(2):
                d = dest_ref[j * n_tok + t0 + r]
                pltpu.make_async_copy(ys_hbm.at[pl.ds(d, 1), :], ybuf[par][j].at[pl.ds(r, 1), :],
                                      sem.at[par]).start()

    def wait(par):
        for j in range(2):
            pltpu.make_async_copy(ys_hbm.at[pl.ds(0, tm), :], ybuf[par][j], sem.at[par]).wait()

    def combine(par):
        rows = slice(par * tm, (par + 1) * tm)
        w = wts_ref[rows, :]
        out_ref[rows, :] = x1_ref[rows, :] + (w[:, 0:1] * ybuf[par][0][...] + w[:, 1:2] * ybuf[par][1][...])

    @pl.when(i == 0)
    def _():
        issue(0, 0)

    issue(2 * i + 1, 1)
    wait(0)
    combine(0)

    @pl.when(2 * i + 2 < nblk)
    def _():
        issue(2 * i + 2, 0)

    wait(1)
    combine(1)


def _combine(x1, ys, dest, wts, tm):
    S = x1.shape[0]
    nblk = S // tm
    assert nblk % 2 == 0
    return pl.pallas_call(
        functools.partial(_combine_kernel, tm=tm, nblk=nblk, n_tok=S),
        grid_spec=pltpu.PrefetchScalarGridSpec(
            num_scalar_prefetch=1,
            grid=(nblk // 2,),
            in_specs=[
                pl.BlockSpec((2 * tm, D_MODEL), lambda i, dest: (i, 0)),
                pl.BlockSpec((2 * tm, LANES), lambda i, dest: (i, 0)),
                pl.BlockSpec(memory_space=pl.ANY),
            ],
            out_specs=pl.BlockSpec((2 * tm, D_MODEL), lambda i, dest: (i, 0)),
            scratch_shapes=[pltpu.VMEM((tm, D_MODEL), F32)] * 4 + [pltpu.SemaphoreType.DMA((2,))],
        ),
        out_shape=jax.ShapeDtypeStruct((S, D_MODEL), F32),
        compiler_params=_params(("arbitrary",), disable_bounds_checks=True),
        name="moe_combine",
    )(dest, x1, wts, ys)


def _moe(xn2, x1, ids, wts, w_gate, w_up, w_down):
    tm = MOE_TM
    order, tile_expert, tile_base, n_active, dest = _route_plan(ids[:, :2], tm)
    ys = _moe_tiles(xn2, order, tile_expert, tile_base, n_active, w_gate, w_up, w_down, tm)
    return _combine(x1, ys, dest, wts, _tile(x1.shape[0], MOE_TM))


def _tile(S, want):
    return min(want, S)


def _layer(x, p):
    S = x.shape[0]
    proj, gates = _inproj(x, p["g1"], p["w_head"], p["w_br"], p["w_gate_in"], p["b_gate_in"], p["qk_gain"],
                          _tile(S, 1024))
    y_na = _na_attention(proj, p["na_bias"])
    q_ml = _conv_silu(proj, p["conv_w"], p["conv_b"], _tile(S, 512), is_key=False)
    kt_ml = _conv_silu(proj, p["conv_w"], p["conv_b"], _tile(S, 512), is_key=True)
    G = 2
    gates_t = jnp.transpose(gates[:, :N_ML_GATES].reshape(S // ML_KCHUNK, ML_KCHUNK, N_ML_GATES), (0, 2, 1))
    hf, hb = _mlstm_scan(q_ml, kt_ml, proj, gates_t, G)
    merged = _merge(y_na, hf, hb, proj, p["b_branch"], p["hnorm_g"], p["w_na_out"], p["w_ml_out"], _tile(S, 512))
    x1, xn2, ids, wts = _outproj(merged, x, p["w_o"], p["g2"], p["w_router"], p["b_router"], _tile(S, 512))
    return _moe(xn2, x1, ids, wts, p["w_gate"], p["w_up"], p["w_down"])


def _prepare(l, norm1_g, w_in, b_ml_gates, b_branch, qn_g, kn_g, na_rpb, ml_conv_w, ml_conv_b, ml_hnorm_g,
             w_na_out, w_ml_out, w_o, norm2_g, w_router_group, b_router_group, w_router_expert,
             b_router_expert, w_gate, w_up, w_down):
    g_lo = 3 * NA_WIDTH + 4 * ML_WIDTH
    g_hi = g_lo + N_ML_GATES
    w = w_in[l]
    pad = LANES - N_ML_GATES
    n_r = N_GROUPS + N_EXPERTS
    w_r = jnp.pad(jnp.concatenate([w_router_group[l], w_router_expert[l]], axis=1), ((0, 0), (0, LANES - n_r)))
    col_blocks = lambda a: jnp.transpose(a.reshape(a.shape[0], -1, COL_BLOCK), (1, 0, 2))
    w_r_hi = w_r.astype(BF16)
    w_r_lo = (w_r - w_r_hi.astype(F32)).astype(BF16)
    return {
        "g1": norm1_g[l].reshape(1, D_MODEL),
        "w_head": col_blocks(w[:, :g_lo].astype(BF16)),
        "w_br": col_blocks(w[:, g_hi:].astype(BF16)),
        "w_gate_in": jnp.pad(w[:, g_lo:g_hi], ((0, 0), (0, pad))).astype(BF16),
        "b_gate_in": jnp.pad(b_ml_gates[l], (0, pad)).reshape(1, LANES),
        "qk_gain": jnp.concatenate([qn_g[l].reshape(-1), kn_g[l].reshape(-1)]).reshape(1, 2 * NA_WIDTH),
        "na_bias": _na_bias_table(na_rpb[l]),
        "conv_w": ml_conv_w[l],
        "conv_b": ml_conv_b[l].reshape(1, 2 * ML_WIDTH),
        "b_branch": b_branch[l].reshape(1, 2 * D_MODEL),
        "hnorm_g": ml_hnorm_g[l].reshape(1, ML_WIDTH),
        "w_na_out": w_na_out[l].astype(BF16),
        "w_ml_out": w_ml_out[l].astype(BF16),
        "w_o": w_o[l].astype(BF16),
        "g2": norm2_g[l].reshape(1, D_MODEL),
        "w_router": jnp.concatenate([w_r_hi, w_r_lo], axis=1),
        "b_router": jnp.pad(jnp.concatenate([b_router_group[l], b_router_expert[l]]), (0, LANES - n_r)).reshape(1, LANES),
        "w_gate": w_gate[l].astype(BF16),
        "w_up": w_up[l].astype(BF16),
        "w_down": w_down[l].astype(BF16),
    }


def kernel(x_prompt, x_sample, norm1_g, w_in, b_ml_gates, b_branch, qn_g, kn_g, na_rpb, ml_conv_w, ml_conv_b,
           ml_hnorm_g, w_na_out, w_ml_out, w_o, norm2_g, w_router_group, b_router_group, w_router_expert,
           b_router_expert, w_gate, w_up, w_down):
    depth = w_in.shape[0]
    layers = [_prepare(l, norm1_g, w_in, b_ml_gates, b_branch, qn_g, kn_g, na_rpb, ml_conv_w, ml_conv_b,
                       ml_hnorm_g, w_na_out, w_ml_out, w_o, norm2_g, w_router_group, b_router_group,
                       w_router_expert, b_router_expert, w_gate, w_up, w_down) for l in range(depth)]

    def trunk(x):
        b, s, d = x.shape
        outs = []
        for bi in range(b):
            h = x.reshape(s, d) if b == 1 else x[bi]
            for p in layers:
                h = _layer(h, p)
            outs.append(h)
        return outs[0].reshape(b, s, d) if b == 1 else jnp.stack(outs)

    return (trunk(x_prompt), trunk(x_sample))
```

```python
import functools

import numpy as np
import jax
import jax.numpy as jnp
from jax import lax
from jax.experimental import pallas as pl
from jax.experimental.pallas import tpu as pltpu

F32 = jnp.float32
BF16 = jnp.bfloat16

D_MODEL = 2048
GRID_W = 64
NA_HEADS = 8
NA_HEAD_DIM = 128
NA_WIDTH = NA_HEADS * NA_HEAD_DIM
NA_ROWS = 8
NA_COLS = 16
ML_HEADS = 4
ML_HEAD_DIM = 256
ML_WIDTH = ML_HEADS * ML_HEAD_DIM
ML_CHUNK = 64
N_ML_GATES = 4 * ML_HEADS
N_GROUPS = 4
EXPERTS_PER_GROUP = 4
N_EXPERTS = N_GROUPS * EXPERTS_PER_GROUP
D_EXPERT = 512
EPS = 1e-6

LANES = 128
COL_BLOCK = 1024
N_MAIN = 3 * NA_WIDTH + 4 * ML_WIDTH + 2 * D_MODEL
CB_NA_Q, CB_NA_K, CB_NA_V, CB_ML_Q, CB_ML_K, CB_ML_V, CB_ML_O, CB_BR = 0, 1, 2, 3, 4, 5, 6, 7
NEG_BIG = -1e30
VMEM_LIMIT = 56 * 1024 * 1024


def _sigmoid(x):
    return 0.5 * jnp.tanh(0.5 * x) + 0.5


def _params(sem, vmem=VMEM_LIMIT, **kw):
    return pltpu.CompilerParams(dimension_semantics=sem, vmem_limit_bytes=vmem, **kw)


def _inproj_kernel(x_ref, g1_ref, wa_ref, wb_ref, wg_ref, bg_ref, qkg_ref, proj_ref, gates_ref, xn_ref):
    j = pl.program_id(1)

    @pl.when(j == 0)
    def _():
        x = x_ref[...]
        ms = jnp.mean(x * x, axis=-1, keepdims=True)
        xn = ((x * lax.rsqrt(ms + EPS)) * g1_ref[...]).astype(BF16)
        xn_ref[...] = xn
        gates_ref[...] = jnp.dot(xn, wg_ref[...], preferred_element_type=F32) + bg_ref[...]

    @pl.when(j <= CB_NA_K)
    def _():
        acc = jnp.dot(xn_ref[...], wa_ref[...], preferred_element_type=F32)
        for h in range(NA_HEADS):
            sl = slice(h * NA_HEAD_DIM, (h + 1) * NA_HEAD_DIM)
            a = acc[:, sl]
            ms = jnp.mean(a * a, axis=-1, keepdims=True)
            proj_ref[:, sl] = ((a * lax.rsqrt(ms + EPS)) * qkg_ref[:, sl]).astype(BF16)

    @pl.when((j > CB_NA_K) & (j < CB_BR))
    def _():
        proj_ref[...] = jnp.dot(xn_ref[...], wa_ref[...], preferred_element_type=F32).astype(BF16)

    @pl.when(j >= CB_BR)
    def _():
        proj_ref[...] = jnp.dot(xn_ref[...], wb_ref[...], preferred_element_type=F32).astype(BF16)


def _inproj(x, g1, w_head, w_br, w_gate, b_gate, qk_gain, tm):
    S = x.shape[0]
    nj = N_MAIN // COL_BLOCK
    return pl.pallas_call(
        _inproj_kernel,
        grid=(S // tm, nj),
        in_specs=[
            pl.BlockSpec((tm, D_MODEL), lambda i, j: (i, 0)),
            pl.BlockSpec((1, D_MODEL), lambda i, j: (0, 0)),
            pl.BlockSpec((D_MODEL, COL_BLOCK), lambda i, j: (0, jnp.minimum(j, CB_BR - 1))),
            pl.BlockSpec((D_MODEL, COL_BLOCK), lambda i, j: (0, jnp.maximum(j - CB_BR, 0))),
            pl.BlockSpec((D_MODEL, LANES), lambda i, j: (0, 0)),
            pl.BlockSpec((1, LANES), lambda i, j: (0, 0)),
            pl.BlockSpec((1, COL_BLOCK), lambda i, j: (0, jnp.minimum(j, CB_NA_K))),
        ],
        out_specs=[
            pl.BlockSpec((tm, COL_BLOCK), lambda i, j: (i, j)),
            pl.BlockSpec((tm, LANES), lambda i, j: (i, 0)),
        ],
        out_shape=[
            jax.ShapeDtypeStruct((S, N_MAIN), BF16),
            jax.ShapeDtypeStruct((S, LANES), F32),
        ],
        scratch_shapes=[pltpu.VMEM((tm, D_MODEL), BF16)],
        compiler_params=_params(("parallel", "arbitrary")),
        name="inproj",
    )(x, g1, w_head, w_br, w_gate, b_gate, qk_gain)


NA_QROWS = 8
NA_QTOK = NA_QROWS * GRID_W
NA_KTOK = NA_ROWS * GRID_W


def _na_bias_table(rpb):
    c = np.arange(GRID_W)
    start_c = np.clip(c - NA_COLS // 2, 0, GRID_W - NA_COLS)
    kc = np.arange(GRID_W)
    mask = (kc[None, :] >= start_c[:, None]) & (kc[None, :] < start_c[:, None] + NA_COLS)
    dc = kc[None, :] - c[:, None] + NA_COLS - 1
    onehot = ((dc[None] == np.arange(2 * NA_COLS - 1)[:, None, None]) & mask[None]).astype(np.float32)
    e = jnp.einsum("hrd,dck->hrck", rpb.astype(F32), onehot, precision=lax.Precision.HIGHEST)
    e = jnp.where(mask[None, None], e, NEG_BIG)
    b = jnp.stack([e[:, NA_ROWS - 1 - d:2 * NA_ROWS - 1 - d] for d in range(NA_ROWS)], axis=0)
    b = jnp.transpose(b, (0, 1, 3, 2, 4))
    return b.reshape(NA_ROWS * NA_HEADS, GRID_W, NA_KTOK)


def _na_kernel(q_ref, kp_ref, kc_ref, kn_ref, vp_ref, vc_ref, vn_ref, bias_ref, o_ref, kw_ref, vw_ref,
               s_ref, p_ref, l_ref, *, rows):
    b = pl.program_id(0)
    kw_ref[0:NA_QTOK] = kp_ref[...]
    kw_ref[NA_QTOK:2 * NA_QTOK] = kc_ref[...]
    kw_ref[2 * NA_QTOK:3 * NA_QTOK] = kn_ref[...]
    vw_ref[0:NA_QTOK] = vp_ref[...]
    vw_ref[NA_QTOK:2 * NA_QTOK] = vc_ref[...]
    vw_ref[2 * NA_QTOK:3 * NA_QTOK] = vn_ref[...]
    scale = NA_HEAD_DIM ** -0.5

    def row_body(rl, carry):
        r = b * NA_QROWS + rl
        rs = jnp.clip(r - NA_ROWS // 2, 0, rows - NA_ROWS)
        d = r - rs
        koff = pl.multiple_of((rs - (b - 1) * NA_QROWS) * GRID_W, GRID_W)
        qoff = pl.multiple_of(rl * GRID_W, GRID_W)
        heads = [slice(h * NA_HEAD_DIM, (h + 1) * NA_HEAD_DIM) for h in range(NA_HEADS)]
        for h, sl in enumerate(heads):
            q = q_ref[pl.ds(qoff, GRID_W), sl]
            k = kw_ref[pl.ds(koff, NA_KTOK), sl]
            s = lax.dot_general(q, k, (((1,), (1,)), ((), ())), preferred_element_type=F32)
            s_ref[h] = s * scale + bias_ref[d * NA_HEADS + h]
        for h in range(NA_HEADS):
            s = s_ref[h]
            p = jnp.exp(s - jnp.max(s, axis=-1, keepdims=True))
            l_ref[h] = jnp.sum(p, axis=-1, keepdims=True)
            p_ref[h] = p.astype(BF16)
        for h, sl in enumerate(heads):
            v = vw_ref[pl.ds(koff, NA_KTOK), sl]
            o = jnp.dot(p_ref[h], v, preferred_element_type=F32)
            o_ref[pl.ds(qoff, GRID_W), sl] = (o / l_ref[h]).astype(BF16)
        return carry

    lax.fori_loop(0, NA_QROWS, row_body, 0, unroll=True)


def _na_attention(proj, bias_tab):
    S = proj.shape[0]
    rows = S // GRID_W
    nb = rows // NA_QROWS
    blk = (NA_QTOK, COL_BLOCK)
    prev = lambda b: jnp.maximum(b - 1, 0)
    nxt = lambda b: jnp.minimum(b + 1, nb - 1)
    return pl.pallas_call(
        functools.partial(_na_kernel, rows=rows),
        grid=(nb,),
        in_specs=[
            pl.BlockSpec(blk, lambda b: (b, CB_NA_Q)),
            pl.BlockSpec(blk, lambda b: (prev(b), CB_NA_K)),
            pl.BlockSpec(blk, lambda b: (b, CB_NA_K)),
            pl.BlockSpec(blk, lambda b: (nxt(b), CB_NA_K)),
            pl.BlockSpec(blk, lambda b: (prev(b), CB_NA_V)),
            pl.BlockSpec(blk, lambda b: (b, CB_NA_V)),
            pl.BlockSpec(blk, lambda b: (nxt(b), CB_NA_V)),
            pl.BlockSpec((NA_ROWS * NA_HEADS, GRID_W, NA_KTOK), lambda b: (0, 0, 0)),
        ],
        out_specs=pl.BlockSpec(blk, lambda b: (b, 0)),
        out_shape=jax.ShapeDtypeStruct((S, NA_WIDTH), BF16),
        scratch_shapes=[
            pltpu.VMEM((3 * NA_QTOK, NA_WIDTH), BF16),
            pltpu.VMEM((3 * NA_QTOK, NA_WIDTH), BF16),
            pltpu.VMEM((NA_HEADS, GRID_W, NA_KTOK), F32),
            pltpu.VMEM((NA_HEADS, GRID_W, NA_KTOK), BF16),
            pltpu.VMEM((NA_HEADS, GRID_W, 1), F32),
        ],
        compiler_params=_params(("parallel",)),
        name="na_attention",
    )(proj, proj, proj, proj, proj, proj, proj, bias_tab)


HALO = 16


ML_KCHUNK = 256


def _conv_kernel(x_ref, xp_ref, xn_ref, w_ref, b_ref, o_ref, *, tb, nblk, is_key):
    i = pl.program_id(0)
    x = x_ref[...].astype(F32)
    prev_row = jnp.where(i > 0, xp_ref[HALO - 1:HALO, :].astype(F32), 0.0)
    next_row = jnp.where(i < nblk - 1, xn_ref[0:1, :].astype(F32), 0.0)
    ridx = lax.broadcasted_iota(jnp.int32, (tb, 1), 0)
    x_prev = jnp.where(ridx == 0, prev_row, pltpu.roll(x, 1, 0))
    x_next = jnp.where(ridx == tb - 1, next_row, pltpu.roll(x, tb - 1, 0))
    y = b_ref[...] + x_prev * w_ref[0:1, :] + x * w_ref[1:2, :] + x_next * w_ref[2:3, :]
    y = y * _sigmoid(y)
    if is_key:
        y = y * (ML_HEAD_DIM ** -0.5)
        for j in range(tb // ML_KCHUNK):
            o_ref[j] = y[j * ML_KCHUNK:(j + 1) * ML_KCHUNK, :].T.astype(BF16)
    else:
        o_ref[...] = y.astype(BF16)


def _conv_silu(proj, conv_w, conv_b, tb, is_key):
    S = proj.shape[0]
    nblk = S // tb
    hb = tb // HALO
    c = 1 if is_key else 0
    if is_key:
        out_spec = pl.BlockSpec((tb // ML_KCHUNK, ML_WIDTH, ML_KCHUNK), lambda i: (i, 0, 0))
        out_shape = jax.ShapeDtypeStruct((S // ML_KCHUNK, ML_WIDTH, ML_KCHUNK), BF16)
    else:
        out_spec = pl.BlockSpec((tb, ML_WIDTH), lambda i: (i, 0))
        out_shape = jax.ShapeDtypeStruct((S, ML_WIDTH), BF16)
    return pl.pallas_call(
        functools.partial(_conv_kernel, tb=tb, nblk=nblk, is_key=is_key),
        grid=(nblk,),
        in_specs=[
            pl.BlockSpec((tb, COL_BLOCK), lambda i: (i, CB_ML_Q + c)),
            pl.BlockSpec((HALO, COL_BLOCK), lambda i: (jnp.maximum(i * hb - 1, 0), CB_ML_Q + c)),
            pl.BlockSpec((HALO, COL_BLOCK), lambda i: (jnp.minimum((i + 1) * hb, S // HALO - 1), CB_ML_Q + c)),
            pl.BlockSpec((3, COL_BLOCK), lambda i: (0, c)),
            pl.BlockSpec((1, COL_BLOCK), lambda i: (0, c)),
        ],
        out_specs=out_spec,
        out_shape=out_shape,
        compiler_params=_params(("parallel",)),
        name="conv_silu_k" if is_key else "conv_silu_q",
    )(proj, proj, proj, conv_w, conv_b)


ML_AUG = ML_HEAD_DIM + LANES


def _log_sigmoid(x):
    return jnp.minimum(x, 0.0) - jnp.log1p(jnp.exp(-jnp.abs(x)))


def _split3(x):
    x1 = x.astype(BF16)
    r1 = x - x1.astype(F32)
    x2 = r1.astype(BF16)
    x3 = (r1 - x2.astype(F32)).astype(BF16)
    return x1, x2, x3


def _mlstm_kernel(qf_ref, ktf_ref, vf_ref, grf_ref, qb_ref, ktb_ref, vb_ref, grb_ref,
                  hf_ref, hb_ref, c_ref, m_ref, *, G):
    @pl.when(pl.program_id(0) == 0)
    def _():
        c_ref[...] = jnp.zeros_like(c_ref)
        m_ref[...] = jnp.zeros_like(m_ref)

    L = ML_KCHUNK
    ri = lax.broadcasted_iota(jnp.int32, (L, L), 0)
    ci = lax.broadcasted_iota(jnp.int32, (L, L), 1)
    tril = ci <= ri
    triu = ci >= ri
    trilb = tril.astype(BF16)
    triub = triu.astype(BF16)
    ones_blk = jnp.ones((L, LANES), BF16)
    pad_rows = jnp.zeros((LANES - N_ML_GATES, L), F32)

    def setup(dirn, cc):
        if dirn == 0:
            q_ref, kt_ref, v_ref, gr_ref, h_ref = qf_ref, ktf_ref, vf_ref, grf_ref, hf_ref
            tri_m, mask = triub, tril
        else:
            q_ref, kt_ref, v_ref, gr_ref, h_ref = qb_ref, ktb_ref, vb_ref, grb_ref, hb_ref
            tri_m, mask = trilb, triu
        grow = gr_ref[cc]
        br = jnp.dot(jnp.concatenate(_split3(_log_sigmoid(grow)), axis=0), tri_m, preferred_element_type=F32)
        brow_all = br[:N_ML_GATES] + br[N_ML_GATES:2 * N_ML_GATES] + br[2 * N_ML_GATES:]
        bcol_all = jnp.concatenate([brow_all, pad_rows], axis=0).T
        return dict(dirn=dirn, cc=cc, toff=pl.multiple_of(cc * L, L), q_ref=q_ref, kt_ref=kt_ref, v_ref=v_ref,
                    h_ref=h_ref, mask=mask, grow=grow, brow_all=brow_all, bcol_all=bcol_all)

    def operands(ctx, h):
        hs = slice(h * ML_HEAD_DIM, (h + 1) * ML_HEAD_DIM)
        q = ctx["q_ref"][pl.ds(ctx["toff"], L), hs]
        kt = ctx["kt_ref"][ctx["cc"], hs, :]
        v = ctx["v_ref"][pl.ds(ctx["toff"], L), hs]
        vaug = jnp.concatenate([v, ones_blk], axis=1)
        return q, kt, vaug

    def gate_rows(ctx, h):
        ioff = ctx["dirn"] * 2 * ML_HEADS
        foff = ioff + ML_HEADS
        brow = ctx["brow_all"][foff + h:foff + h + 1, :]
        irow = ctx["grow"][ioff + h:ioff + h + 1, :]
        bcol = ctx["bcol_all"][:, foff + h:foff + h + 1]
        return brow, irow, bcol

    def scores(ctx, h):
        ch = ctx["dirn"] * ML_HEADS + h
        m = m_ref[ch:ch + 1, 0:1]
        brow, irow, bcol = gate_rows(ctx, h)
        log_d = jnp.where(ctx["mask"], bcol - brow + irow, NEG_BIG)
        m_t = jnp.maximum(bcol + m, jnp.max(log_d, axis=-1, keepdims=True))
        dmat = jnp.exp(log_d - m_t)
        inter = jnp.exp(bcol + m - m_t)
        q, kt, _ = operands(ctx, h)
        s = jnp.dot(q, kt, preferred_element_type=F32)
        return (s * dmat).astype(BF16), inter, m_t

    def outputs(ctx, h, sd, inter, m_t):
        ch = ctx["dirn"] * ML_HEADS + h
        hs = slice(h * ML_HEAD_DIM, (h + 1) * ML_HEAD_DIM)
        q, _, vaug = operands(ctx, h)
        num = (jnp.dot(sd, vaug, preferred_element_type=F32)
               + inter * jnp.dot(q, c_ref[ch].astype(BF16), preferred_element_type=F32))
        den = num[:, ML_HEAD_DIM:ML_HEAD_DIM + 1]
        ctx["h_ref"][pl.ds(ctx["toff"], L), hs] = (num[:, :ML_HEAD_DIM]
                                                   / jnp.maximum(jnp.abs(den), jnp.exp(-m_t))).astype(BF16)

    def state_update(ctx, h):
        ch = ctx["dirn"] * ML_HEADS + h
        m = m_ref[ch:ch + 1, 0:1]
        brow, irow, _ = gate_rows(ctx, h)
        g = brow[:, L - 1:L] if ctx["dirn"] == 0 else brow[:, 0:1]
        log_w = g - brow + irow
        m_new = jnp.maximum(g + m, jnp.max(log_w, axis=-1, keepdims=True))
        w = jnp.exp(log_w - m_new)
        decay = jnp.exp(g + m - m_new)
        _, kt, vaug = operands(ctx, h)
        kw = (kt.astype(F32) * w).astype(BF16)
        upd = jnp.dot(kw, vaug, preferred_element_type=F32)
        c_ref[ch] = decay * c_ref[ch] + upd
        m_ref[ch:ch + 1, :] = jnp.broadcast_to(m_new, (1, LANES))

    def chunk_body(c, carry):
        chains = [(ctx, h) for ctx in (setup(0, c), setup(1, G - 1 - c)) for h in range(ML_HEADS)]
        scored = [scores(ctx, h) for ctx, h in chains]
        for (ctx, h), (sd, inter, m_t) in zip(chains, scored):
            outputs(ctx, h, sd, inter, m_t)
        for ctx, h in chains:
            state_update(ctx, h)
        return carry

    lax.fori_loop(0, G, chunk_body, 0)


def _mlstm_scan(q, kt, proj, gates_t, G):
    S = q.shape[0]
    tb = G * ML_KCHUNK
    nb = S // tb
    fwd = lambda i: i
    bwd = lambda i: nb - 1 - i
    in_specs = []
    for mp in (fwd, bwd):
        in_specs += [
            pl.BlockSpec((tb, ML_WIDTH), lambda i, mp=mp: (mp(i), 0)),
            pl.BlockSpec((G, ML_WIDTH, ML_KCHUNK), lambda i, mp=mp: (mp(i), 0, 0)),
            pl.BlockSpec((tb, COL_BLOCK), lambda i, mp=mp: (mp(i), CB_ML_V)),
            pl.BlockSpec((G, N_ML_GATES, ML_KCHUNK), lambda i, mp=mp: (mp(i), 0, 0)),
        ]
    return pl.pallas_call(
        functools.partial(_mlstm_kernel, G=G),
        grid=(nb,),
        in_specs=in_specs,
        out_specs=[
            pl.BlockSpec((tb, ML_WIDTH), lambda i: (fwd(i), 0)),
            pl.BlockSpec((tb, ML_WIDTH), lambda i: (bwd(i), 0)),
        ],
        out_shape=[jax.ShapeDtypeStruct((S, ML_WIDTH), BF16), jax.ShapeDtypeStruct((S, ML_WIDTH), BF16)],
        scratch_shapes=[
            pltpu.VMEM((2 * ML_HEADS, ML_HEAD_DIM, ML_AUG), F32),
            pltpu.VMEM((2 * ML_HEADS, LANES), F32),
        ],
        compiler_params=_params(("arbitrary",)),
        name="mlstm_scan",
    )(q, kt, proj, gates_t, q, kt, proj, gates_t)


MERGE_NB = D_MODEL // COL_BLOCK


def _merge_kernel(yna_ref, hf_ref, hb_ref, o_ref, *rest):
    g_refs = rest[:2 * MERGE_NB]
    bias_ref, hg_ref, wna_ref, wml_ref, out_ref, yml_ref = rest[2 * MERGE_NB:]
    for h in range(ML_HEADS):
        sl = slice(h * ML_HEAD_DIM, (h + 1) * ML_HEAD_DIM)
        hh = hf_ref[:, sl].astype(F32) + hb_ref[:, sl].astype(F32)
        ms = jnp.mean(hh * hh, axis=-1, keepdims=True)
        y = (hh * lax.rsqrt(ms + EPS)) * hg_ref[:, sl]
        yml_ref[:, sl] = (y * _sigmoid(o_ref[:, sl].astype(F32))).astype(BF16)
    for n in range(MERGE_NB):
        cols = slice(n * COL_BLOCK, (n + 1) * COL_BLOCK)
        ga = _sigmoid(g_refs[n][...].astype(F32) + bias_ref[:, cols])
        gb = _sigmoid(g_refs[MERGE_NB + n][...].astype(F32) + bias_ref[:, D_MODEL + n * COL_BLOCK:
                                                                       D_MODEL + (n + 1) * COL_BLOCK])
        a = jnp.dot(yna_ref[...], wna_ref[:, cols], preferred_element_type=F32)
        bm = jnp.dot(yml_ref[...], wml_ref[:, cols], preferred_element_type=F32)
        out_ref[:, cols] = (ga * a + gb * bm).astype(BF16)


def _merge(y_na, hf, hb, proj, b_branch, hnorm_g, w_na_out, w_ml_out, tm):
    S = y_na.shape[0]
    const = lambda shape: pl.BlockSpec(shape, lambda i: (0, 0), pipeline_mode=pl.Buffered(1))
    gate_specs = [pl.BlockSpec((tm, COL_BLOCK), lambda i, c=c: (i, CB_BR + c)) for c in range(2 * MERGE_NB)]
    return pl.pallas_call(
        _merge_kernel,
        grid=(S // tm,),
        in_specs=[
            pl.BlockSpec((tm, NA_WIDTH), lambda i: (i, 0)),
            pl.BlockSpec((tm, ML_WIDTH), lambda i: (i, 0)),
            pl.BlockSpec((tm, ML_WIDTH), lambda i: (i, 0)),
            pl.BlockSpec((tm, COL_BLOCK), lambda i: (i, CB_ML_O)),
        ] + gate_specs + [
            const((1, 2 * D_MODEL)),
            const((1, ML_WIDTH)),
            const((NA_WIDTH, D_MODEL)),
            const((ML_WIDTH, D_MODEL)),
        ],
        out_specs=pl.BlockSpec((tm, D_MODEL), lambda i: (i, 0)),
        out_shape=jax.ShapeDtypeStruct((S, D_MODEL), BF16),
        scratch_shapes=[pltpu.VMEM((tm, ML_WIDTH), BF16)],
        compiler_params=_params(("parallel",)),
        name="merge",
    )(y_na, hf, hb, proj, *([proj] * (2 * MERGE_NB)), b_branch, hnorm_g, w_na_out, w_ml_out)


def _outproj_kernel(mg_ref, x_ref, wo_ref, g2_ref, wr_ref, br_ref, x1_ref, xn_ref, ids_ref, wts_ref):
    x1 = x_ref[...] + jnp.dot(mg_ref[...], wo_ref[...], preferred_element_type=F32)
    x1_ref[...] = x1
    ms = jnp.mean(x1 * x1, axis=-1, keepdims=True)
    xn = (x1 * lax.rsqrt(ms + EPS)) * g2_ref[...]
    xn_ref[...] = xn
    xh = xn.astype(BF16)
    xl = (xn - xh.astype(F32)).astype(BF16)
    hl = jnp.dot(xh, wr_ref[...], preferred_element_type=F32)
    logits = (hl[:, :LANES] + hl[:, LANES:]
              + jnp.dot(xl, wr_ref[:, :LANES], preferred_element_type=F32) + br_ref[...])
    tm = logits.shape[0]
    lane = lax.broadcasted_iota(jnp.int32, (tm, LANES), 1)
    big = jnp.int32(1 << 20)
    gmask = lane < N_GROUPS
    lg = jnp.where(gmask, logits, NEG_BIG)
    gmax = jnp.max(lg, axis=-1, keepdims=True)
    p_group = 1.0 / jnp.sum(jnp.exp(lg - gmax), axis=-1, keepdims=True)
    g_idx = jnp.min(jnp.where(lg == gmax, lane, big), axis=-1, keepdims=True)
    e_lo = N_GROUPS + g_idx * EXPERTS_PER_GROUP
    emask = (lane >= e_lo) & (lane < e_lo + EXPERTS_PER_GROUP)
    le = jnp.where(emask, logits, NEG_BIG)
    m1 = jnp.max(le, axis=-1, keepdims=True)
    i1 = jnp.min(jnp.where(le == m1, lane, big), axis=-1, keepdims=True)
    le2 = jnp.where(lane == i1, NEG_BIG, le)
    m2 = jnp.max(le2, axis=-1, keepdims=True)
    i2 = jnp.min(jnp.where(le2 == m2, lane, big), axis=-1, keepdims=True)
    e2 = jnp.exp(m2 - m1)
    w1 = p_group / (1.0 + e2)
    w2 = p_group * e2 / (1.0 + e2)
    ids_ref[...] = jnp.where(lane == 0, i1 - N_GROUPS, jnp.where(lane == 1, i2 - N_GROUPS, 0))
    wts_ref[...] = jnp.where(lane == 0, w1, jnp.where(lane == 1, w2, 0.0))


def _outproj(merged, x, w_o, g2, w_router, b_router, tm):
    S = x.shape[0]
    return pl.pallas_call(
        _outproj_kernel,
        grid=(S // tm,),
        in_specs=[
            pl.BlockSpec((tm, D_MODEL), lambda i: (i, 0)),
            pl.BlockSpec((tm, D_MODEL), lambda i: (i, 0)),
            pl.BlockSpec((D_MODEL, D_MODEL), lambda i: (0, 0), pipeline_mode=pl.Buffered(1)),
            pl.BlockSpec((1, D_MODEL), lambda i: (0, 0)),
            pl.BlockSpec((D_MODEL, 2 * LANES), lambda i: (0, 0)),
            pl.BlockSpec((1, LANES), lambda i: (0, 0)),
        ],
        out_specs=[
            pl.BlockSpec((tm, D_MODEL), lambda i: (i, 0)),
            pl.BlockSpec((tm, D_MODEL), lambda i: (i, 0)),
            pl.BlockSpec((tm, LANES), lambda i: (i, 0)),
            pl.BlockSpec((tm, LANES), lambda i: (i, 0)),
        ],
        out_shape=[
            jax.ShapeDtypeStruct((S, D_MODEL), F32),
            jax.ShapeDtypeStruct((S, D_MODEL), F32),
            jax.ShapeDtypeStruct((S, LANES), jnp.int32),
            jax.ShapeDtypeStruct((S, LANES), F32),
        ],
        compiler_params=_params(("parallel",)),
        name="outproj_router",
    )(merged, x, w_o, g2, w_router, b_router)


MOE_TM = 256


def _route_plan(ids, tm):
    n_tok = ids.shape[0]
    n_pairs = 2 * n_tok
    nt = n_pairs // tm + N_EXPERTS
    assert n_pairs & (n_pairs - 1) == 0 and nt * tm <= (1 << 16), "pair ids / rows are packed into 16-bit fields"
    e = ids.T.reshape(-1)
    pid = jnp.arange(n_pairs, dtype=jnp.int32)
    order = lax.sort(e * n_pairs + pid) & (n_pairs - 1)
    experts = jnp.arange(N_EXPERTS, dtype=jnp.int32)
    cnt = jnp.sum((e[None, :] == experts[:, None]).astype(jnp.int32), axis=1)
    tiles_e = (cnt + tm - 1) // tm
    tend = jnp.cumsum(tiles_e)
    tstart = tend - tiles_e
    cend = jnp.cumsum(cnt)
    cstart = cend - cnt
    tile_idx = jnp.arange(nt, dtype=jnp.int32)
    te = jnp.minimum(jnp.sum((tile_idx[:, None] >= tend[None, :]).astype(jnp.int32), axis=1), N_EXPERTS - 1)
    local = (tile_idx - tstart[te]) * tm
    base = jnp.where(cnt[te] - local > 0, cstart[te] + local, 0)
    off = tstart * tm - cstart
    steps = jnp.where(pid[None, :] >= cend[:-1, None], (off[1:] - off[:-1])[:, None], 0)
    row_of_pos = pid + off[0] + jnp.sum(steps, axis=0)
    dest = lax.sort(order.astype(jnp.uint32) * jnp.uint32(1 << 16) + row_of_pos.astype(jnp.uint32))
    dest = (dest & jnp.uint32(0xFFFF)).astype(jnp.int32)
    return jnp.concatenate([order, pid[:tm]]), te.astype(jnp.int32), base.astype(jnp.int32), tend[-1:], dest


def _moe_tile_kernel(te_ref, base_ref, nact_ref, order_ref, xn_hbm, wg0_ref, wu0_ref, wd0_ref, wg1_ref, wu1_ref,
                     wd1_ref, y_ref, gbuf0, gbuf1, gsem, *, tm, n_tok):
    k = pl.program_id(0)
    n_active = nact_ref[0]
    gbuf = (gbuf0, gbuf1)
    wts = ((wg0_ref, wu0_ref, wd0_ref), (wg1_ref, wu1_ref, wd1_ref))

    def issue_gather(tile, par):
        first = base_ref[tile]
        for r in range(tm):
            tok = order_ref[first + r] & (n_tok - 1)
            pltpu.make_async_copy(xn_hbm.at[pl.ds(tok, 1), :], gbuf[par].at[pl.ds(r, 1), :], gsem.at[par]).start()

    def wait_gather(par):
        pltpu.make_async_copy(xn_hbm.at[pl.ds(0, tm), :], gbuf[par], gsem.at[par]).wait()

    def half_step(tile, par):
        rows = slice(par * tm, (par + 1) * tm)

        @pl.when(tile < n_active)
        def _():
            wait_gather(par)

        def compute():
            wg_ref, wu_ref, wd_ref = wts[par]
            xg = gbuf[par][...].astype(BF16)
            hg = jnp.dot(xg, wg_ref[0], preferred_element_type=F32)
            hu = jnp.dot(xg, wu_ref[0], preferred_element_type=F32)
            hh = ((hg * _sigmoid(hg)) * hu).astype(BF16)
            y_ref[rows, :] = jnp.dot(hh, wd_ref[0], preferred_element_type=F32)

        @pl.when(tile + 1 < n_active)
        def _():
            issue_gather(tile + 1, 1 - par)

        @pl.when(tile < n_active)
        def _():
            compute()

        @pl.when(tile >= n_active)
        def _():
            y_ref[rows, :] = jnp.zeros((tm, D_MODEL), F32)

    @pl.when((k == 0) & (n_active > 0))
    def _():
        issue_gather(0, 0)

    half_step(2 * k, 0)
    half_step(2 * k + 1, 1)


def _moe_tiles(xn2, order, tile_expert, tile_base, n_active, w_gate, w_up, w_down, tm):
    n_tok = xn2.shape[0]
    assert n_tok & (n_tok - 1) == 0, "token count must be a power of two (pair id -> token by masking)"
    nt = tile_expert.shape[0]
    assert nt % 2 == 0

    def wspec(shape, par):
        return pl.BlockSpec(shape, lambda k, te, base, nact, order: (te[2 * k + par], 0, 0))

    w_specs = [wspec(s, par) for par in (0, 1)
               for s in ((1, D_MODEL, D_EXPERT), (1, D_MODEL, D_EXPERT), (1, D_EXPERT, D_MODEL))]
    return pl.pallas_call(
        functools.partial(_moe_tile_kernel, tm=tm, n_tok=n_tok),
        grid_spec=pltpu.PrefetchScalarGridSpec(
            num_scalar_prefetch=4,
            grid=(nt // 2,),
            in_specs=[pl.BlockSpec(memory_space=pl.ANY)] + w_specs,
            out_specs=pl.BlockSpec((2 * tm, D_MODEL), lambda k, te, base, nact, order: (k, 0)),
            scratch_shapes=[
                pltpu.VMEM((tm, D_MODEL), F32),
                pltpu.VMEM((tm, D_MODEL), F32),
                pltpu.SemaphoreType.DMA((2,)),
            ],
        ),
        out_shape=jax.ShapeDtypeStruct((nt * tm, D_MODEL), F32),
        compiler_params=_params(("arbitrary",), disable_bounds_checks=True),
        name="moe_tiles",
    )(tile_expert, tile_base, n_active, order, xn2, w_gate, w_up, w_down, w_gate, w_up, w_down)


def _combine_kernel(dest_ref, x1_ref, wts_ref, ys_hbm, out_ref, ya0, yb0, ya1, yb1, sem, *, tm, nblk, n_tok):
    i = pl.program_id(0)
    ybuf = ((ya0, yb0), (ya1, yb1))

    def issue(blk, par):
        t0 = blk * tm
        for r in range(tm):
            for j in range(2):
                d = dest_ref[j * n_tok + t0 + r]
                pltpu.make_async_copy(ys_hbm.at[pl.ds(d, 1), :], ybuf[par][j].at[pl.ds(r, 1), :],
                                      sem.at[par]).start()

    def wait(par):
        for j in range(2):
            pltpu.make_async_copy(ys_hbm.at[pl.ds(0, tm), :], ybuf[par][j], sem.at[par]).wait()

    def combine(par):
        rows = slice(par * tm, (par + 1) * tm)
        w = wts_ref[rows, :]
        out_ref[rows, :] = x1_ref[rows, :] + (w[:, 0:1] * ybuf[par][0][...] + w[:, 1:2] * ybuf[par][1][...])

    @pl.when(i == 0)
    def _():
        issue(0, 0)

    issue(2 * i + 1, 1)
    wait(0)
    combine(0)

    @pl.when(2 * i + 2 < nblk)
    def _():
        issue(2 * i + 2, 0)

    wait(1)
    combine(1)


def _combine(x1, ys, dest, wts, tm):
    S = x1.shape[0]
    nblk = S // tm
    assert nblk % 2 == 0
    return pl.pallas_call(
        functools.partial(_combine_kernel, tm=tm, nblk=nblk, n_tok=S),
        grid_spec=pltpu.PrefetchScalarGridSpec(
            num_scalar_prefetch=1,
            grid=(nblk // 2,),
            in_specs=[
                pl.BlockSpec((2 * tm, D_MODEL), lambda i, dest: (i, 0)),
                pl.BlockSpec((2 * tm, LANES), lambda i, dest: (i, 0)),
                pl.BlockSpec(memory_space=pl.ANY),
            ],
            out_specs=pl.BlockSpec((2 * tm, D_MODEL), lambda i, dest: (i, 0)),
            scratch_shapes=[pltpu.VMEM((tm, D_MODEL), F32)] * 4 + [pltpu.SemaphoreType.DMA((2,))],
        ),
        out_shape=jax.ShapeDtypeStruct((S, D_MODEL), F32),
        compiler_params=_params(("arbitrary",), disable_bounds_checks=True),
        name="moe_combine",
    )(dest, x1, wts, ys)


def _moe(xn2, x1, ids, wts, w_gate, w_up, w_down):
    tm = MOE_TM
    order, tile_expert, tile_base, n_active, dest = _route_plan(ids[:, :2], tm)
    ys = _moe_tiles(xn2, order, tile_expert, tile_base, n_active, w_gate, w_up, w_down, tm)
    return _combine(x1, ys, dest, wts, _tile(x1.shape[0], MOE_TM))


def _tile(S, want):
    return min(want, S)


def _layer(x, p):
    S = x.shape[0]
    proj, gates = _inproj(x, p["g1"], p["w_head"], p["w_br"], p["w_gate_in"], p["b_gate_in"], p["qk_gain"],
                          _tile(S, 1024))
    y_na = _na_attention(proj, p["na_bias"])
    q_ml = _conv_silu(proj, p["conv_w"], p["conv_b"], _tile(S, 512), is_key=False)
    kt_ml = _conv_silu(proj, p["conv_w"], p["conv_b"], _tile(S, 512), is_key=True)
    G = 2
    gates_t = jnp.transpose(gates[:, :N_ML_GATES].reshape(S // ML_KCHUNK, ML_KCHUNK, N_ML_GATES), (0, 2, 1))
    hf, hb = _mlstm_scan(q_ml, kt_ml, proj, gates_t, G)
    merged = _merge(y_na, hf, hb, proj, p["b_branch"], p["hnorm_g"], p["w_na_out"], p["w_ml_out"], _tile(S, 512))
    x1, xn2, ids, wts = _outproj(merged, x, p["w_o"], p["g2"], p["w_router"], p["b_router"], _tile(S, 512))
    return _moe(xn2, x1, ids, wts, p["w_gate"], p["w_up"], p["w_down"])


def _prepare(l, norm1_g, w_in, b_ml_gates, b_branch, qn_g, kn_g, na_rpb, ml_conv_w, ml_conv_b, ml_hnorm_g,
             w_na_out, w_ml_out, w_o, norm2_g, w_router_group, b_router_group, w_router_expert,
             b_router_expert, w_gate, w_up, w_down):
    g_lo = 3 * NA_WIDTH + 4 * ML_WIDTH
    g_hi = g_lo + N_ML_GATES
    w = w_in[l]
    pad = LANES - N_ML_GATES
    n_r = N_GROUPS + N_EXPERTS
    w_r = jnp.pad(jnp.concatenate([w_router_group[l], w_router_expert[l]], axis=1), ((0, 0), (0, LANES - n_r)))
    w_r_hi = w_r.astype(BF16)
    w_r_lo = (w_r - w_r_hi.astype(F32)).astype(BF16)
    return {
        "g1": norm1_g[l].reshape(1, D_MODEL),
        "w_head": w[:, :g_lo].astype(BF16),
        "w_br": w[:, g_hi:].astype(BF16),
        "w_gate_in": jnp.pad(w[:, g_lo:g_hi], ((0, 0), (0, pad))).astype(BF16),
        "b_gate_in": jnp.pad(b_ml_gates[l], (0, pad)).reshape(1, LANES),
        "qk_gain": jnp.concatenate([qn_g[l].reshape(-1), kn_g[l].reshape(-1)]).reshape(1, 2 * NA_WIDTH),
        "na_bias": _na_bias_table(na_rpb[l]),
        "conv_w": ml_conv_w[l],
        "conv_b": ml_conv_b[l].reshape(1, 2 * ML_WIDTH),
        "b_branch": b_branch[l].reshape(1, 2 * D_MODEL),
        "hnorm_g": ml_hnorm_g[l].reshape(1, ML_WIDTH),
        "w_na_out": w_na_out[l].astype(BF16),
        "w_ml_out": w_ml_out[l].astype(BF16),
        "w_o": w_o[l].astype(BF16),
        "g2": norm2_g[l].reshape(1, D_MODEL),
        "w_router": jnp.concatenate([w_r_hi, w_r_lo], axis=1),
        "b_router": jnp.pad(jnp.concatenate([b_router_group[l], b_router_expert[l]]), (0, LANES - n_r)).reshape(1, LANES),
        "w_gate": w_gate[l].astype(BF16),
        "w_up": w_up[l].astype(BF16),
        "w_down": w_down[l].astype(BF16),
    }


def kernel(x_prompt, x_sample, norm1_g, w_in, b_ml_gates, b_branch, qn_g, kn_g, na_rpb, ml_conv_w, ml_conv_b,
           ml_hnorm_g, w_na_out, w_ml_out, w_o, norm2_g, w_router_group, b_router_group, w_router_expert,
           b_router_expert, w_gate, w_up, w_down):
    depth = w_in.shape[0]
    layers = [_prepare(l, norm1_g, w_in, b_ml_gates, b_branch, qn_g, kn_g, na_rpb, ml_conv_w, ml_conv_b,
                       ml_hnorm_g, w_na_out, w_ml_out, w_o, norm2_g, w_router_group, b_router_group,
                       w_router_expert, b_router_expert, w_gate, w_up, w_down) for l in range(depth)]

    def trunk(x):
        b, s, d = x.shape
        outs = []
        for bi in range(b):
            h = x.reshape(s, d) if b == 1 else x[bi]
            for p in layers:
                h = _layer(h, p)
            outs.append(h)
        return outs[0].reshape(b, s, d) if b == 1 else jnp.stack(outs)

    return (trunk(x_prompt), trunk(x_sample))
```
